```python
import math
import jax
import jax.numpy as jnp
from jax import lax
import numpy as np

D_MODEL = 2048
BATCH = 4
SEQ = 4096
DEPTH = 4

GRID_W = 64
CTX_LEN = 256
N_MIXERS = 4
D_FF = 4 * D_MODEL
Q_BLOCK = 128
ROPE_BASE = 10000.0
EPS = 1e-6
DN_ALPHA = (2.0 * DEPTH) ** 0.25
DN_BETA = (8.0 * DEPTH) ** -0.25
ADA_SCALE = 0.5
A_HEADS = D_MODEL // 128
A_DK = 64
A_DV = 2 * A_DK
B_HEADS = D_MODEL // 128
B_KV_HEADS = 4
B_GROUP = B_HEADS // B_KV_HEADS
B_HD = 128
B_IN_DIM = D_MODEL + 2 * B_KV_HEADS * B_HD
C_DI = 2 * D_MODEL
C_HD = 64
C_HEADS = C_DI // C_HD
C_GROUPS = 8
C_HPG = C_HEADS // C_GROUPS
C_STATE = 128
C_CONV = 5
C_CHUNK = 128
C_CONV_DIM = C_DI + 2 * C_GROUPS * C_STATE
C_IN_DIM = C_DI + C_CONV_DIM + 2 * C_HEADS
D_HEADS = D_MODEL // 128
D_HD = 128
NA_ROWS_MAX = 8
NA_COLS = 16

kernel_name = "hybrid_diffusion_backbone_interleaved"

F32 = jnp.float32


def n_of_mixer(m):
    return len(range(m, DEPTH, N_MIXERS))


def layer_norm(x, g, b):
    x32 = x.astype(F32)
    mu = jnp.mean(x32, axis=-1, keepdims=True)
    var = jnp.mean(jnp.square(x32 - mu), axis=-1, keepdims=True)
    return ((x32 - mu) * lax.rsqrt(var + EPS) * g.astype(F32) + b.astype(F32)).astype(x.dtype)


def rms_norm(x, g):
    x32 = x.astype(F32)
    y = x32 * lax.rsqrt(jnp.mean(x32 * x32, axis=-1, keepdims=True) + EPS)
    return (y * g.astype(F32)).astype(x.dtype)


def axial_rope_angles(n, dim):
    t = jnp.arange(n, dtype=jnp.int32)
    row = (t // GRID_W).astype(F32)
    col = (t % GRID_W).astype(F32)
    n_pairs = dim // 4
    inv = ROPE_BASE ** (-jnp.arange(n_pairs, dtype=F32) / n_pairs)
    ang = jnp.concatenate([row[:, None] * inv, col[:, None] * inv], axis=-1)
    return jnp.cos(ang), jnp.sin(ang)


def apply_rope(x, cos, sin):
    half = x.shape[-1] // 2
    shp = (1, x.shape[1]) + (1,) * (x.ndim - 3) + (half,)
    cs, sn = cos.reshape(shp), sin.reshape(shp)
    xp = x.astype(F32).reshape(x.shape[:-1] + (half, 2))
    x1, x2 = xp[..., 0], xp[..., 1]
    out = jnp.stack([x1 * cs - x2 * sn, x1 * sn + x2 * cs], axis=-1).reshape(x.shape)
    return out.astype(x.dtype)


def sweep_query_blocks(fn, q):
    b, n = q.shape[0], q.shape[1]
    nb = n // Q_BLOCK
    qb = jnp.moveaxis(q.reshape((b, nb, Q_BLOCK) + q.shape[2:]), 1, 0)
    out = jnp.moveaxis(lax.map(fn, qb), 0, 1)
    return out.reshape((b, n) + out.shape[3:])


def diff_attention(u_lat, u_ctx, w_in, lam_vec, sub_g, w_out, cos, sin, lam_init, need_ctx):
    def proj(u):
        b, n, _ = u.shape
        qkv = u @ w_in
        q = qkv[..., :D_MODEL].reshape(b, n, A_HEADS, 2, A_DK)
        k = qkv[..., D_MODEL:2 * D_MODEL].reshape(b, n, A_HEADS, 2, A_DK)
        v = qkv[..., 2 * D_MODEL:].reshape(b, n, A_HEADS, A_DV)
        return q, k, v
    q_l, k_l, v_l = proj(u_lat)
    q_c, k_c, v_c = proj(u_ctx)
    q_l, k_l = apply_rope(q_l, cos, sin), apply_rope(k_l, cos, sin)
    lv = lam_vec.astype(F32)
    lam = jnp.exp(jnp.sum(lv[0] * lv[1])) - jnp.exp(jnp.sum(lv[2] * lv[3])) + lam_init
    scale = A_DK ** -0.5

    def attend(k, v):
        def fn(qb):
            s = jnp.einsum('bqhmd,bkhmd->bhmqk', qb, k).astype(F32) * scale
            p = jax.nn.softmax(s, axis=-1)
            a = (p[:, :, 0] - lam * p[:, :, 1]).astype(v.dtype)
            return jnp.einsum('bhqk,bkhd->bqhd', a, v)
        return fn

    def post(o):
        o = rms_norm(o, sub_g) * (1.0 - lam_init)
        return o.reshape(o.shape[0], o.shape[1], D_MODEL) @ w_out

    k_all = jnp.concatenate([k_c, k_l], axis=1)
    v_all = jnp.concatenate([v_c, v_l], axis=1)
    y_l = post(sweep_query_blocks(attend(k_all, v_all), q_l))
    y_c = post(attend(k_c, v_c)(q_c)) if need_ctx else None
    return y_l, y_c


def gqa_attention(u_lat, u_ctx, w_in, qn_g, kn_g, w_out, cos, sin, need_ctx):
    def proj(u):
        b, n, _ = u.shape
        qkv = u @ w_in
        q = rms_norm(qkv[..., :D_MODEL].reshape(b, n, B_HEADS, B_HD), qn_g)
        k = rms_norm(qkv[..., D_MODEL:D_MODEL + B_KV_HEADS * B_HD].reshape(b, n, B_KV_HEADS, B_HD), kn_g)
        v = qkv[..., D_MODEL + B_KV_HEADS * B_HD:].reshape(b, n, B_KV_HEADS, B_HD)
        return q, k, v
    q_l, k_l, v_l = proj(u_lat)
    q_c, k_c, v_c = proj(u_ctx)
    q_l, k_l = apply_rope(q_l, cos, sin), apply_rope(k_l, cos, sin)
    scale = B_HD ** -0.5

    def attend(k, v):
        def fn(qb):
            s = jnp.einsum('bqhgd,bkhd->bhgqk', qb, k).astype(F32) * scale
            p = jax.nn.softmax(s, axis=-1).astype(v.dtype)
            return jnp.einsum('bhgqk,bkhd->bqhgd', p, v)
        return fn

    def grp(q):
        return q.reshape(q.shape[0], q.shape[1], B_KV_HEADS, B_GROUP, B_HD)

    def post(o):
        return o.reshape(o.shape[0], o.shape[1], D_MODEL) @ w_out

    k_all = jnp.concatenate([k_c, k_l], axis=1)
    v_all = jnp.concatenate([v_c, v_l], axis=1)
    y_l = post(sweep_query_blocks(attend(k_all, v_all), grp(q_l)))
    y_c = post(attend(k_c, v_c)(grp(q_c))) if need_ctx else None
    return y_l, y_c


def depthwise_conv_centred(x, w, b):
    k = w.shape[0]
    y = lax.conv_general_dilated(x, w[:, None, :], window_strides=(1,), padding=[(k // 2, k // 2)],
                                 dimension_numbers=('NWC', 'WIO', 'NWC'), feature_group_count=x.shape[-1])
    return y + b


def ssd_chunk_scan(x, dt, a, bm, cm, h0, want_y):
    b, n = x.shape[0], x.shape[1]
    nc = n // C_CHUNK

    def chunked(t):
        return jnp.moveaxis(t.reshape((b, nc, C_CHUNK) + t.shape[2:]), 1, 0)

    tri = jnp.tril(jnp.ones((C_CHUNK, C_CHUNK), dtype=bool))

    def step(h, xs):
        xc, dtc, bc, cc = xs
        cs = jnp.moveaxis(jnp.cumsum(dtc * a, axis=1), 1, -1)
        xdt = xc * dtc[..., None]
        cs_end = cs[..., -1:]
        h_new = h * jnp.exp(cs_end)[..., None] + jnp.einsum('bkgn,bghk,bkghp->bghpn', bc, jnp.exp(cs_end - cs), xdt)
        if not want_y:
            return h_new, None
        lmat = jnp.exp(jnp.where(tri, cs[..., :, None] - cs[..., None, :], -jnp.inf))
        cb = jnp.einsum('bqgn,bkgn->bgqk', cc, bc)
        y = jnp.einsum('bgqk,bghqk,bkghp->bqghp', cb, lmat, xdt)
        y = y + jnp.einsum('bqgn,bghpn,bghq->bqghp', cc, h, jnp.exp(cs))
        return h_new, y

    h, ys = lax.scan(step, h0, (chunked(x), chunked(dt), chunked(bm), chunked(cm)))
    if not want_y:
        return None, h
    return jnp.moveaxis(ys, 0, 1).reshape(x.shape), h


def mamba2_ssd(u_lat, u_ctx, w_in, conv_w, conv_b, a_log, dt_bias, d_skip, norm_g, w_out, need_ctx):
    def prep(u):
        b, n, _ = u.shape
        zxbcdt = u @ w_in
        z = zxbcdt[..., :C_DI]
        xbc = jax.nn.silu(depthwise_conv_centred(zxbcdt[..., C_DI:C_DI + C_CONV_DIM], conv_w, conv_b)).astype(F32)
        dt_raw = zxbcdt[..., C_DI + C_CONV_DIM:].astype(F32).reshape(b, n, 2, C_GROUPS, C_HPG)
        dt = jax.nn.softplus(dt_raw + dt_bias.astype(F32).reshape(2, C_GROUPS, C_HPG))
        xs = xbc[..., :C_DI].reshape(b, n, C_GROUPS, C_HPG, C_HD)
        bm = xbc[..., C_DI:C_DI + C_GROUPS * C_STATE].reshape(b, n, C_GROUPS, C_STATE)
        cm = xbc[..., C_DI + C_GROUPS * C_STATE:].reshape(b, n, C_GROUPS, C_STATE)
        return z, xs, bm, cm, dt

    a = -jnp.exp(a_log.astype(F32)).reshape(2, C_GROUPS, C_HPG)
    dsk = d_skip.astype(F32).reshape(2, C_GROUPS, C_HPG)

    def scan_dir(xs, bm, cm, dt, d, h0, want_y):
        dtd = dt[:, :, d]
        if d == 1:
            xs, bm, cm, dtd = (jnp.flip(t, axis=1) for t in (xs, bm, cm, dtd))
        y, h = ssd_chunk_scan(xs, dtd, a[d], bm, cm, h0, want_y)
        if want_y:
            y = y + dsk[d][..., None] * xs
            if d == 1:
                y = jnp.flip(y, axis=1)
        return y, h

    z_l, x_l, b_l, c_l, dt_l = prep(u_lat)
    z_c, x_c, b_c, c_c, dt_c = prep(u_ctx)
    bsz = u_lat.shape[0]
    h_zero = jnp.zeros((bsz, C_GROUPS, C_HPG, C_HD, C_STATE), F32)
    y_l = None
    y_c = None
    for d in range(2):
        yc, hc = scan_dir(x_c, b_c, c_c, dt_c, d, h_zero, need_ctx)
        yl, _ = scan_dir(x_l, b_l, c_l, dt_l, d, hc, True)
        y_l = yl if y_l is None else y_l + yl
        if need_ctx:
            y_c = yc if y_c is None else y_c + yc

    def post(y, z):
        b, n = y.shape[0], y.shape[1]
        g = rms_norm(y.reshape(b, n, C_DI) * jax.nn.silu(z.astype(F32)), norm_g)
        return g.astype(u_lat.dtype) @ w_out

    return post(y_l, z_l), (post(y_c, z_c) if need_ctx else None)


def neighbourhood_attention(u_lat, u_ctx, w_in, rpb, w_out, need_ctx):
    def proj(u):
        b, n, _ = u.shape
        qkv = u @ w_in
        return tuple(qkv[..., i * D_MODEL:(i + 1) * D_MODEL].reshape(b, n, D_HEADS, D_HD) for i in range(3))
    q_l, k_l, v_l = proj(u_lat)
    q_c, k_c, v_c = proj(u_ctx)
    bsz, n = u_lat.shape[0], u_lat.shape[1]
    rows = n // GRID_W
    kr = min(NA_ROWS_MAX, rows)
    nwin = kr * NA_COLS
    scale = D_HD ** -0.5
    kg = k_l.reshape(bsz, rows, GRID_W, D_HEADS, D_HD)
    vg = v_l.reshape(bsz, rows, GRID_W, D_HEADS, D_HD)
    qg = q_l.reshape(bsz, rows, GRID_W, D_HEADS, D_HD)
    col = jnp.arange(GRID_W, dtype=jnp.int32)
    col_start = jnp.clip(col - NA_COLS // 2, 0, GRID_W - NA_COLS)
    col_idx = col_start[:, None] + jnp.arange(NA_COLS, dtype=jnp.int32)
    col_bias_idx = col_idx - col[:, None] + (NA_COLS - 1)
    rpb32 = rpb.astype(F32)

    def row_fn(args):
        r, q_row = args
        r_start = jnp.clip(r - kr // 2, 0, rows - kr)
        k_band = lax.dynamic_slice_in_dim(kg, r_start, kr, axis=1)
        v_band = lax.dynamic_slice_in_dim(vg, r_start, kr, axis=1)
        k_sel = k_band[:, :, col_idx]
        v_sel = v_band[:, :, col_idx]
        row_bias_idx = r_start + jnp.arange(kr, dtype=jnp.int32) - r + (NA_ROWS_MAX - 1)
        bias = rpb32[:, row_bias_idx[:, None, None], col_bias_idx[None]]
        s_win = jnp.einsum('bqhd,brqkhd->bhqrk', q_row, k_sel).astype(F32) * scale + jnp.transpose(bias, (0, 2, 1, 3))[None]
        s_ctx = jnp.einsum('bqhd,bkhd->bhqk', q_row, k_c).astype(F32) * scale
        s = jnp.concatenate([s_win.reshape(bsz, D_HEADS, GRID_W, nwin), s_ctx], axis=-1)
        p = jax.nn.softmax(s, axis=-1).astype(v_l.dtype)
        p_win = p[..., :nwin].reshape(bsz, D_HEADS, GRID_W, kr, NA_COLS)
        return jnp.einsum('bhqrk,brqkhd->bqhd', p_win, v_sel) + jnp.einsum('bhqk,bkhd->bqhd', p[..., nwin:], v_c)

    o = lax.map(row_fn, (jnp.arange(rows, dtype=jnp.int32), jnp.moveaxis(qg, 1, 0)))
    y_l = jnp.moveaxis(o, 0, 1).reshape(bsz, n, D_MODEL) @ w_out
    y_c = None
    if need_ctx:
        s = jnp.einsum('bqhd,bkhd->bhqk', q_c, k_c).astype(F32) * scale
        p = jax.nn.softmax(s, axis=-1).astype(v_c.dtype)
        o_c = jnp.einsum('bhqk,bkhd->bqhd', p, v_c)
        y_c = o_c.reshape(o_c.shape[0], o_c.shape[1], D_MODEL) @ w_out
    return y_l, y_c


def sq_relu_mlp(u, w1, w2):
    h = jax.nn.relu(u @ w1)
    return (h * h) @ w2


def setup_inputs(seed: int = 0) -> dict:
    key = jax.random.key(seed)
    ks = iter(jax.random.split(key, 40))

    def nrm(shape, s):
        return jax.random.normal(next(ks), shape, F32) * s

    n_a, n_b, n_c, n_d = (n_of_mixer(m) for m in range(N_MIXERS))
    d = D_MODEL
    dt0 = jnp.exp(jax.random.uniform(next(ks), (n_c, 2, C_HEADS), F32, math.log(1e-3), math.log(1e-1)))
    return {
        'x': nrm((BATCH, SEQ, d), 1.0),
        'c': nrm((BATCH, d), 1.0),
        'ctx': nrm((BATCH, CTX_LEN, d), 1.0),
        'c_ctx': nrm((d,), 1.0),
        'ada_w': nrm((DEPTH, d, 6 * d), ADA_SCALE * d ** -0.5),
        'ada_b': nrm((DEPTH, 6 * d), 0.02),
        'ln_g': 1.0 + nrm((DEPTH, 2, d), 0.02),
        'ln_b': nrm((DEPTH, 2, d), 0.02),
        'mlp_w1': nrm((DEPTH, d, D_FF), d ** -0.5),
        'mlp_w2': nrm((DEPTH, D_FF, d), DN_BETA * D_FF ** -0.5),
        'a_w_in': nrm((n_a, d, 3 * d), d ** -0.5),
        'a_lambda': nrm((n_a, 4, A_DK), 0.1),
        'a_sub_g': 1.0 + nrm((n_a, A_DV), 0.02),
        'a_w_out': nrm((n_a, d, d), DN_BETA * d ** -0.5),
        'b_w_in': nrm((n_b, d, B_IN_DIM), d ** -0.5),
        'b_qn_g': 1.0 + nrm((n_b, B_HD), 0.02),
        'b_kn_g': 1.0 + nrm((n_b, B_HD), 0.02),
        'b_w_out': nrm((n_b, d, d), DN_BETA * d ** -0.5),
        'c_w_in': nrm((n_c, d, C_IN_DIM), d ** -0.5),
        'c_conv_w': nrm((n_c, C_CONV, C_CONV_DIM), C_CONV ** -0.5),
        'c_conv_b': nrm((n_c, C_CONV_DIM), 0.02),
        'c_A_log': jnp.log(jax.random.uniform(next(ks), (n_c, 2, C_HEADS), F32, 1.0, 16.0)),
        'c_dt_bias': dt0 + jnp.log(-jnp.expm1(-dt0)),
        'c_D': 1.0 + nrm((n_c, 2, C_HEADS), 0.1),
        'c_norm_g': 1.0 + nrm((n_c, C_DI), 0.02),
        'c_w_out': nrm((n_c, C_DI, d), DN_BETA * C_DI ** -0.5),
        'd_w_in': nrm((n_d, d, 3 * d), d ** -0.5),
        'd_rpb': nrm((n_d, D_HEADS, 2 * NA_ROWS_MAX - 1, 2 * NA_COLS - 1), 0.02),
        'd_w_out': nrm((n_d, d, d), DN_BETA * d ** -0.5),
    }


def reference(x, c, ctx, c_ctx, ada_w, ada_b, ln_g, ln_b, mlp_w1, mlp_w2,
              a_w_in, a_lambda, a_sub_g, a_w_out,
              b_w_in, b_qn_g, b_kn_g, b_w_out,
              c_w_in, c_conv_w, c_conv_b, c_A_log, c_dt_bias, c_D, c_norm_g, c_w_out,
              d_w_in, d_rpb, d_w_out):
    n = x.shape[1]
    cos_a, sin_a = axial_rope_angles(n, A_DK)
    cos_b, sin_b = axial_rope_angles(n, B_HD)
    x_lat, x_ctx = x, ctx
    for i in range(DEPTH):
        m, j = i % N_MIXERS, i // N_MIXERS
        last = i == DEPTH - 1
        mod_l = jnp.split(jax.nn.silu(c) @ ada_w[i] + ada_b[i], 6, axis=-1)
        mod_c = jnp.split(jax.nn.silu(c_ctx) @ ada_w[i] + ada_b[i], 6, axis=-1)
        u_l = x_lat * (1.0 + mod_l[1][:, None]) + mod_l[0][:, None]
        u_c = x_ctx * (1.0 + mod_c[1]) + mod_c[0]
        if m == 0:
            y_l, y_c = diff_attention(u_l, u_c, a_w_in[j], a_lambda[j], a_sub_g[j], a_w_out[j], cos_a, sin_a,
                                      0.8 - 0.6 * math.exp(-0.3 * i), not last)
        elif m == 1:
            y_l, y_c = gqa_attention(u_l, u_c, b_w_in[j], b_qn_g[j], b_kn_g[j], b_w_out[j], cos_b, sin_b, not last)
        elif m == 2:
            y_l, y_c = mamba2_ssd(u_l, u_c, c_w_in[j], c_conv_w[j], c_conv_b[j], c_A_log[j], c_dt_bias[j], c_D[j],
                                  c_norm_g[j], c_w_out[j], not last)
        else:
            y_l, y_c = neighbourhood_attention(u_l, u_c, d_w_in[j], d_rpb[j], d_w_out[j], not last)
        x_lat = layer_norm(DN_ALPHA * x_lat + mod_l[2][:, None] * y_l, ln_g[i, 0], ln_b[i, 0])
        u_l = x_lat * (1.0 + mod_l[4][:, None]) + mod_l[3][:, None]
        x_lat = layer_norm(DN_ALPHA * x_lat + mod_l[5][:, None] * sq_relu_mlp(u_l, mlp_w1[i], mlp_w2[i]),
                           ln_g[i, 1], ln_b[i, 1])
        if not last:
            x_ctx = layer_norm(DN_ALPHA * x_ctx + mod_c[2] * y_c, ln_g[i, 0], ln_b[i, 0])
            u_c = x_ctx * (1.0 + mod_c[4]) + mod_c[3]
            x_ctx = layer_norm(DN_ALPHA * x_ctx + mod_c[5] * sq_relu_mlp(u_c, mlp_w1[i], mlp_w2[i]),
                               ln_g[i, 1], ln_b[i, 1])
    return x_lat
```

```python
import functools
import math

import jax
import jax.numpy as jnp
from jax import lax
from jax.experimental import pallas as pl
from jax.experimental.pallas import tpu as pltpu

F32 = jnp.float32
BF16 = jnp.bfloat16

D_MODEL = 2048
DEPTH = 4
N_MIXERS = 4
GRID_W = 64
D_FF = 4 * D_MODEL
ROPE_BASE = 10000.0
EPS = 1e-6
DN_ALPHA = (2.0 * DEPTH) ** 0.25
N_HEADS = D_MODEL // 128
HEAD = 128
A_DK = 64
B_KV_HEADS = 4
B_GROUP = N_HEADS // B_KV_HEADS
C_DI = 2 * D_MODEL
C_HD = 64
C_GROUPS = 8
C_HPG = (C_DI // C_HD) // C_GROUPS
C_STATE = 128
C_CONV = 5
C_CHUNK = 128
C_CONV_DIM = C_DI + 2 * C_GROUPS * C_STATE
NA_ROWS = 8
NA_COLS = 16
NA_QROWS = 8
NA_WROWS = NA_QROWS + NA_ROWS
MASKED = -1e30

V7X_VMEM_BYTES = 64 * 1024 * 1024
VMEM_LIMIT = V7X_VMEM_BYTES - 8 * 1024 * 1024
LANES = 128
SUBLANES = 8


def _params(*sem):
    return pltpu.CompilerParams(dimension_semantics=sem, vmem_limit_bytes=VMEM_LIMIT)


def _dot(a, b):
    return jnp.dot(a, b, preferred_element_type=F32)


def _dot_nt(a, b, precision=None):
    return lax.dot_general(a, b, (((1,), (1,)), ((), ())), preferred_element_type=F32, precision=precision)


def _silu(x):
    return x / (1.0 + jnp.exp(-x))


def _ada_kernel(c_ref, w_ref, b_ref, o_ref):
    sc = _silu(c_ref[...]).astype(BF16)
    o_ref[...] = _dot(sc, w_ref[...].astype(BF16)) + b_ref[...]


def _ada_mods(cvec, ada_w, ada_b):
    depth, d, n = ada_w.shape
    bn = 1024
    out = pl.pallas_call(
        _ada_kernel,
        grid=(depth, n // bn),
        in_specs=[
            pl.BlockSpec((SUBLANES, d), lambda l, j: (0, 0)),
            pl.BlockSpec((None, d, bn), lambda l, j: (l, 0, j)),
            pl.BlockSpec((None, 1, bn), lambda l, j: (l, 0, j)),
        ],
        out_specs=pl.BlockSpec((None, SUBLANES, bn), lambda l, j: (l, 0, j)),
        out_shape=jax.ShapeDtypeStruct((depth, SUBLANES, n), F32),
        compiler_params=_params("parallel", "parallel"),
        name="ada_mods",
    )(cvec, ada_w, ada_b.reshape(depth, 1, n))
    return out.reshape(depth, SUBLANES, 6, 1, d)


class _Rows:
    def __init__(self, batch, seq, ctx, bm):
        self.batch, self.seq, self.ctx, self.bm = batch, seq, ctx, bm
        self.n_lat = batch * seq
        self.n_all = batch * (seq + ctx)
        assert seq % bm == 0 and (batch * ctx) % bm == 0

    def mod_row(self, i):
        return jnp.minimum((i * self.bm) // self.seq, self.batch)

    def mod_spec(self, mods, layer, k, grid_pos):
        d = mods.shape[-1]
        return pl.BlockSpec((None, None, None, 1, d),
                            lambda *g: (layer, self.mod_row(g[grid_pos]), k, 0, 0))


def _modulate_kernel(x_ref, sh_ref, sc_ref, u_ref):
    u_ref[...] = (x_ref[...] * (1.0 + sc_ref[...]) + sh_ref[...]).astype(u_ref.dtype)


def _modulate(x_all, mods, layer, rows):
    m, d = x_all.shape
    bm = rows.bm
    return pl.pallas_call(
        _modulate_kernel,
        grid=(m // bm,),
        in_specs=[pl.BlockSpec((bm, d), lambda i: (i, 0)),
                  rows.mod_spec(mods, layer, 0, 0), rows.mod_spec(mods, layer, 1, 0)],
        out_specs=pl.BlockSpec((bm, d), lambda i: (i, 0)),
        out_shape=jax.ShapeDtypeStruct((m, d), BF16),
        compiler_params=_params("parallel"),
        name="modulate",
    )(x_all, mods, mods)


def _proj_kernel(*refs, mode, scale, norm):
    if mode == "rope":
        if norm:
            u_ref, w_ref, cos_ref, sin_ref, g_ref, o_ref = refs
        else:
            u_ref, w_ref, cos_ref, sin_ref, o_ref = refs
    else:
        u_ref, w_ref, o_ref = refs
    acc = _dot(u_ref[...], w_ref[...])
    if mode == "relu2":
        r = jnp.maximum(acc, 0.0)
        o_ref[...] = (r * r).astype(o_ref.dtype)
    elif mode == "plain":
        if scale != 1.0:
            acc = acc * scale
        o_ref[...] = acc.astype(o_ref.dtype)
    else:
        cos = cos_ref[...]
        sin = sin_ref[...]
        for c in range(acc.shape[1] // HEAD):
            x = acc[:, c * HEAD:(c + 1) * HEAD]
            if norm:
                x = x * lax.rsqrt(jnp.mean(x * x, axis=1, keepdims=True) + EPS) * g_ref[...]
            x = x * cos + pltpu.roll(x, HEAD // 2, 1) * sin
            if scale != 1.0:
                x = x * scale
            o_ref[:, c * HEAD:(c + 1) * HEAD] = x.astype(o_ref.dtype)


def _proj(u, w, col0, ncols, rows, *, mode="plain", scale=1.0, rope=None, gain=None,
          out_dtype=BF16, bn=1024, n_rows=None):
    m = u.shape[0] if n_rows is None else n_rows
    k = u.shape[1]
    bm = rows.bm
    bn = min(bn, ncols)
    assert ncols % bn == 0 and col0 % bn == 0 and m % bm == 0
    cb0 = col0 // bn
    in_specs = [pl.BlockSpec((bm, k), lambda j, i: (i, 0)),
                pl.BlockSpec((k, bn), lambda j, i: (0, cb0 + j))]
    args = [u, w]
    if mode == "rope":
        cos, sin = rope
        n_seq = rows.seq // bm
        n_lat = rows.n_lat // bm

        def tab(j, i):
            return (jnp.where(i < n_lat, i % n_seq, n_seq), 0)

        in_specs += [pl.BlockSpec((bm, HEAD), tab), pl.BlockSpec((bm, HEAD), tab)]
        args += [cos, sin]
        if gain is not None:
            in_specs.append(pl.BlockSpec((1, HEAD), lambda j, i: (0, 0)))
            args.append(gain)
    return pl.pallas_call(
        functools.partial(_proj_kernel, mode=mode, scale=scale, norm=gain is not None),
        grid=(ncols // bn, m // bm),
        in_specs=in_specs,
        out_specs=pl.BlockSpec((bm, bn), lambda j, i: (i, j)),
        out_shape=jax.ShapeDtypeStruct((m, ncols), out_dtype),
        compiler_params=_params("parallel", "parallel"),
        name="proj_" + mode,
    )(*args)


def _mm_ln_kernel(*refs, nk, emit_u):
    if emit_u:
        a_ref, w_ref, x_ref, gate_ref, g_ref, b_ref, sh_ref, sc_ref, xo_ref, uo_ref = refs[:10]
        rest = refs[10:]
    else:
        a_ref, w_ref, x_ref, gate_ref, g_ref, b_ref, xo_ref = refs[:7]
        rest = refs[7:]

    def finish(y):
        v = DN_ALPHA * x_ref[...] + gate_ref[...] * y
        mu = jnp.mean(v, axis=1, keepdims=True)
        vc = v - mu
        var = jnp.mean(vc * vc, axis=1, keepdims=True)
        o = vc * lax.rsqrt(var + EPS) * g_ref[...] + b_ref[...]
        xo_ref[...] = o
        if emit_u:
            uo_ref[...] = (o * (1.0 + sc_ref[...]) + sh_ref[...]).astype(uo_ref.dtype)

    if nk == 1:
        finish(_dot(a_ref[...], w_ref[...]))
    else:
        acc_ref, = rest
        kk = pl.program_id(1)

        @pl.when(kk == 0)
        def _():
            acc_ref[...] = jnp.zeros_like(acc_ref)

        acc_ref[...] += _dot(a_ref[...], w_ref[...])

        @pl.when(kk == nk - 1)
        def _():
            finish(acc_ref[...])


def _mm_ln(a, w, x_res, mods, layer, gate_k, ln_g, ln_b, rows, *, next_mod=None, n_rows=None,
           bm=512, bk=1024):
    m = a.shape[0] if n_rows is None else n_rows
    k = a.shape[1]
    d = w.shape[1]
    bm = min(bm, rows.bm)
    bk = min(bk, k)
    nk = k // bk
    sub = _Rows(rows.batch, rows.seq, rows.ctx, bm)
    emit_u = next_mod is not None
    vec = pl.BlockSpec((1, d), lambda i, kk: (0, 0))
    in_specs = [pl.BlockSpec((bm, bk), lambda i, kk: (i, kk)),
                pl.BlockSpec((bk, d), lambda i, kk: (kk, 0)),
                pl.BlockSpec((bm, d), lambda i, kk: (i, 0)),
                sub.mod_spec(mods, layer, gate_k, 0), vec, vec]
    args = [a, w, x_res, mods, ln_g.reshape(1, d), ln_b.reshape(1, d)]
    out_specs = [pl.BlockSpec((bm, d), lambda i, kk: (i, 0))]
    out_shape = [jax.ShapeDtypeStruct((m, d), F32)]
    if emit_u:
        nl, ksh, ksc = next_mod
        in_specs += [sub.mod_spec(mods, nl, ksh, 0), sub.mod_spec(mods, nl, ksc, 0)]
        args += [mods, mods]
        out_specs.append(pl.BlockSpec((bm, d), lambda i, kk: (i, 0)))
        out_shape.append(jax.ShapeDtypeStruct((m, d), BF16))
    res = pl.pallas_call(
        functools.partial(_mm_ln_kernel, nk=nk, emit_u=emit_u),
        grid=(m // bm, nk),
        in_specs=in_specs,
        out_specs=out_specs,
        out_shape=out_shape,
        scratch_shapes=[pltpu.VMEM((bm, d), F32)] if nk > 1 else [],
        compiler_params=_params("parallel", "arbitrary"),
        name="mm_ln",
    )(*args)
    return (res[0], res[1]) if emit_u else (res[0], None)


def _attn_kernel(*refs, n_maps, with_lat, lam_init):
    refs = list(refs)
    if n_maps == 2:
        lam_ref = refs.pop(0)
    q_ref, kc_ref, vc_ref = refs[:3]
    refs = refs[3:]
    if with_lat:
        kl_ref, vl_ref = refs[:2]
        refs = refs[2:]
    if n_maps == 2:
        g_ref = refs.pop(0)
    o_ref, = refs

    q = q_ref[...]
    outs = []
    for mi in range(n_maps):
        if n_maps == 2:
            lane = lax.broadcasted_iota(jnp.int32, q.shape, 1)
            qm = jnp.where((lane // 32) % 2 == mi, q, jnp.zeros_like(q))
        else:
            qm = q
        s_c = _dot_nt(qm, kc_ref[...])
        mx = jnp.max(s_c, axis=1, keepdims=True)
        if with_lat:
            s_l = _dot_nt(qm, kl_ref[...])
            mx = jnp.maximum(mx, jnp.max(s_l, axis=1, keepdims=True))
        p_c = jnp.exp(s_c - mx)
        den = jnp.sum(p_c, axis=1, keepdims=True)
        o = _dot(p_c.astype(BF16), vc_ref[...])
        if with_lat:
            p_l = jnp.exp(s_l - mx)
            den = den + jnp.sum(p_l, axis=1, keepdims=True)
            o = o + _dot(p_l.astype(BF16), vl_ref[...])
        outs.append(o / den)
    if n_maps == 2:
        lv = lam_ref[...]
        lam = (jnp.exp(jnp.sum(lv[0:1] * lv[1:2], axis=1, keepdims=True))
               - jnp.exp(jnp.sum(lv[2:3] * lv[3:4], axis=1, keepdims=True)) + lam_init)
        o = outs[0] - lam * outs[1]
        o = o * lax.rsqrt(jnp.mean(o * o, axis=1, keepdims=True) + EPS) * g_ref[...] * (1.0 - lam_init)
    else:
        o = outs[0]
    o_ref[...] = o.astype(o_ref.dtype)


def _attention(q, k, v, rows, *, kv_group, n_maps=1, lam_vec=None, sub_g=None, lam_init=0.0,
               ctx_queries=False, bq=256):
    batch, seq, ctx = rows.batch, rows.seq, rows.ctx
    n_heads = q.shape[1] // HEAD
    lat_blocks = rows.n_lat // ctx
    if ctx_queries:
        bq = ctx
        nq = 1
        q_map = lambda b, h, j: (lat_blocks + b, h)
        m_out = batch * ctx
    else:
        nq = seq // bq
        q_map = lambda b, h, j: (b * nq + j, h)
        m_out = rows.n_lat
    kc_map = lambda b, h, j: (lat_blocks + b, h // kv_group)
    in_specs, args = [], []
    if n_maps == 2:
        in_specs.append(pl.BlockSpec(lam_vec.shape, lambda b, h, j: (0, 0)))
        args.append(lam_vec)
    in_specs += [pl.BlockSpec((bq, HEAD), q_map),
                 pl.BlockSpec((ctx, HEAD), kc_map), pl.BlockSpec((ctx, HEAD), kc_map)]
    args += [q, k, v]
    if not ctx_queries:
        kl_map = lambda b, h, j: (b, h // kv_group)
        in_specs += [pl.BlockSpec((seq, HEAD), kl_map), pl.BlockSpec((seq, HEAD), kl_map)]
        args += [k, v]
    if n_maps == 2:
        in_specs.append(pl.BlockSpec((1, HEAD), lambda b, h, j: (0, 0)))
        args.append(sub_g)
    o_map = (lambda b, h, j: (b, h)) if ctx_queries else (lambda b, h, j: (b * nq + j, h))
    return pl.pallas_call(
        functools.partial(_attn_kernel, n_maps=n_maps, with_lat=not ctx_queries, lam_init=lam_init),
        grid=(batch, n_heads, nq),
        in_specs=in_specs,
        out_specs=pl.BlockSpec((bq, HEAD), o_map),
        out_shape=jax.ShapeDtypeStruct((m_out, n_heads * HEAD), BF16),
        compiler_params=_params("parallel", "parallel", "arbitrary"),
        name="attn_ctx" if ctx_queries else "attn_lat",
    )(*args)


def _na_kernel(q_ref, k_ref, v_ref, kc_ref, vc_ref, bias_ref, o_ref, *, n_rows):
    j = pl.program_id(2)
    wlen = NA_WROWS * GRID_W
    start = pl.multiple_of(jnp.clip(NA_QROWS * j - NA_ROWS // 2, 0, n_rows - NA_WROWS) * GRID_W, 4 * GRID_W)
    q = q_ref[...]
    kw = k_ref[pl.ds(start, wlen), :]
    vw = v_ref[pl.ds(start, wlen), :]
    s_w = _dot_nt(q, kw) + bias_ref[...]
    s_c = _dot_nt(q, kc_ref[...])
    mx = jnp.maximum(jnp.max(s_w, axis=1, keepdims=True), jnp.max(s_c, axis=1, keepdims=True))
    p_w = jnp.exp(s_w - mx)
    p_c = jnp.exp(s_c - mx)
    den = jnp.sum(p_w, axis=1, keepdims=True) + jnp.sum(p_c, axis=1, keepdims=True)
    o = _dot(p_w.astype(BF16), vw) + _dot(p_c.astype(BF16), vc_ref[...])
    o_ref[...] = (o / den).astype(o_ref.dtype)


def _na_bias_table(rpb, n_rows):
    n_blocks = n_rows // NA_QROWS
    tables = []
    a = jnp.arange(NA_QROWS, dtype=jnp.int32)[:, None, None, None]
    qc = jnp.arange(GRID_W, dtype=jnp.int32)[None, :, None, None]
    c = jnp.arange(NA_WROWS, dtype=jnp.int32)[None, None, :, None]
    kc = jnp.arange(GRID_W, dtype=jnp.int32)[None, None, None, :]
    for j in (0, 1, n_blocks - 1):
        ws = min(max(NA_QROWS * j - NA_ROWS // 2, 0), n_rows - NA_WROWS)
        qr = NA_QROWS * j + a
        kr = ws + c
        rs = jnp.clip(qr - NA_ROWS // 2, 0, n_rows - NA_ROWS)
        cs = jnp.clip(qc - NA_COLS // 2, 0, GRID_W - NA_COLS)
        valid = (kr >= rs) & (kr < rs + NA_ROWS) & (kc >= cs) & (kc < cs + NA_COLS)
        ri = jnp.clip(kr - qr + (NA_ROWS - 1), 0, 2 * NA_ROWS - 2)
        ci = jnp.clip(kc - qc + (NA_COLS - 1), 0, 2 * NA_COLS - 2)
        flat = jnp.broadcast_to(ri * (2 * NA_COLS - 1) + ci, valid.shape).reshape(-1)
        vals = jnp.take(rpb.reshape(rpb.shape[0], -1).astype(F32), flat, axis=1)
        vals = jnp.where(valid.reshape(-1)[None], vals, MASKED)
        tables.append(vals.reshape(rpb.shape[0], NA_QROWS * GRID_W, NA_WROWS * GRID_W))
    return jnp.stack(tables, axis=1)


def _neighbourhood(q, k, v, rpb, rows):
    batch, seq, ctx = rows.batch, rows.seq, rows.ctx
    n_rows = seq // GRID_W
    assert n_rows % NA_QROWS == 0 and n_rows >= NA_WROWS
    nb = n_rows // NA_QROWS
    bq = NA_QROWS * GRID_W
    wlen = NA_WROWS * GRID_W
    lat_blocks = rows.n_lat // ctx
    bias = _na_bias_table(rpb, n_rows)
    kind = lambda j: jnp.where(j == 0, 0, jnp.where(j == nb - 1, 2, 1))
    return pl.pallas_call(
        functools.partial(_na_kernel, n_rows=n_rows),
        grid=(N_HEADS, batch, nb),
        in_specs=[pl.BlockSpec((bq, HEAD), lambda h, b, j: (b * nb + j, h)),
                  pl.BlockSpec((seq, HEAD), lambda h, b, j: (b, h)),
                  pl.BlockSpec((seq, HEAD), lambda h, b, j: (b, h)),
                  pl.BlockSpec((ctx, HEAD), lambda h, b, j: (lat_blocks + b, h)),
                  pl.BlockSpec((ctx, HEAD), lambda h, b, j: (lat_blocks + b, h)),
                  pl.BlockSpec((None, None, bq, wlen), lambda h, b, j: (h, kind(j), 0, 0))],
        out_specs=pl.BlockSpec((bq, HEAD), lambda h, b, j: (b * nb + j, h)),
        out_shape=jax.ShapeDtypeStruct((rows.n_lat, N_HEADS * HEAD), BF16),
        compiler_params=_params("parallel", "parallel", "arbitrary"),
        name="na_attn",
    )(q, k, v, k, v, bias)


def _conv_kernel(prev_ref, cur_ref, next_ref, w_ref, b_ref, o_ref, ext_ref, *, blocks_per_seq, n_lat_blocks):
    i = pl.program_id(1)
    bm = cur_ref.shape[0]
    pos = i % blocks_per_seq
    is_lat = i < n_lat_blocks
    first = jnp.logical_or(jnp.logical_not(is_lat), pos == 0)
    last = jnp.logical_or(jnp.logical_not(is_lat), pos == blocks_per_seq - 1)
    halo_p = prev_ref[...].astype(F32)
    halo_n = next_ref[...].astype(F32)
    ext_ref[0:SUBLANES, :] = jnp.where(first, jnp.zeros_like(halo_p), halo_p)
    ext_ref[SUBLANES:SUBLANES + bm, :] = cur_ref[...].astype(F32)
    ext_ref[SUBLANES + bm:2 * SUBLANES + bm, :] = jnp.where(last, jnp.zeros_like(halo_n), halo_n)
    w = w_ref[...]
    acc = jnp.zeros(o_ref.shape, F32) + b_ref[...]
    for t in range(C_CONV):
        off = SUBLANES - C_CONV // 2 + t
        acc = acc + ext_ref[off:off + bm, :] * w[t:t + 1, :]
    o_ref[...] = _silu(acc).astype(o_ref.dtype)


def _conv_silu(xbc, conv_w, conv_b, rows, bc=512):
    m, n = xbc.shape
    bm = rows.ctx
    assert rows.seq % bm == 0 and bm % SUBLANES == 0
    hb = bm // SUBLANES
    n8 = m // SUBLANES
    return pl.pallas_call(
        functools.partial(_conv_kernel, blocks_per_seq=rows.seq // bm, n_lat_blocks=rows.n_lat // bm),
        grid=(n // bc, m // bm),
        in_specs=[pl.BlockSpec((SUBLANES, bc), lambda j, i: (jnp.maximum(i * hb - 1, 0), j)),
                  pl.BlockSpec((bm, bc), lambda j, i: (i, j)),
                  pl.BlockSpec((SUBLANES, bc), lambda j, i: (jnp.minimum((i + 1) * hb, n8 - 1), j)),
                  pl.BlockSpec((C_CONV, bc), lambda j, i: (0, j)),
                  pl.BlockSpec((1, bc), lambda j, i: (0, j))],
        out_specs=pl.BlockSpec((bm, bc), lambda j, i: (i, j)),
        out_shape=jax.ShapeDtypeStruct((m, n), BF16),
        scratch_shapes=[pltpu.VMEM((bm + 2 * SUBLANES, bc), F32)],
        compiler_params=_params("parallel", "arbitrary"),
        name="conv_silu",
    )(xbc, xbc, xbc, conv_w, conv_b.reshape(1, n))


def _ssd_kernel(x_ref, b_ref, c_ref, dt_ref, alog_ref, dtb_ref, dsk_ref, y_ref, h_ref):
    d = pl.program_id(2)
    s = pl.program_id(3)
    q = C_CHUNK

    @pl.when(s == 0)
    def _():
        h_ref[...] = jnp.zeros_like(h_ref)

    row = lax.broadcasted_iota(jnp.int32, (q, q), 0)
    col = lax.broadcasted_iota(jnp.int32, (q, q), 1)
    fwd = d == 0
    before = jnp.where(fwd, row - col, col - row) >= 0
    before_f = before.astype(F32)
    eye = (row == col).astype(F32)
    hi = lax.Precision.HIGHEST

    dt_row = dt_ref[...] + dtb_ref[...]
    dt_row = jnp.maximum(dt_row, 0.0) + jnp.log1p(jnp.exp(-jnp.abs(dt_row)))
    dta_row = dt_row * (-jnp.exp(alog_ref[...]))
    cs_row = _dot_nt(dta_row, before_f, hi)
    cs_col = _dot_nt(before_f, dta_row, hi)
    dt_col = _dot_nt(eye, dt_row, hi)
    end_col = jnp.where(fwd, cs_col[q - 1:q, :], cs_col[0:1, :])

    def expand(t):
        return jnp.concatenate([jnp.broadcast_to(t[:, h:h + 1], (t.shape[0], C_HD)) for h in range(C_HPG)], axis=1)

    cs_e = expand(cs_col)
    end_e = expand(end_col)
    x = x_ref[...].astype(F32)
    xdt = x * expand(dt_col)
    bc = b_ref[...]
    cc = c_ref[...]
    cb = _dot_nt(cc, bc)
    h_old = h_ref[...]
    y = _dot(cc, h_old.astype(BF16)) * jnp.exp(cs_e) + dsk_ref[...] * x
    parts = []
    for h in range(C_HPG):
        decay = jnp.exp(jnp.where(before, cs_col[:, h:h + 1] - cs_row[h:h + 1, :], MASKED))
        parts.append(_dot((cb * decay).astype(BF16), xdt[:, h * C_HD:(h + 1) * C_HD].astype(BF16)))
    y_ref[...] = (y + jnp.concatenate(parts, axis=1)).astype(y_ref.dtype)

    bct = bc.astype(F32).T.astype(BF16)
    h_ref[...] = h_old * jnp.exp(end_e) + _dot(bct, (xdt * jnp.exp(end_e - cs_e)).astype(BF16))


def _ssd(xbc, dt_t, a_log, dt_bias, d_skip, rows):
    batch, seq, ctx = rows.batch, rows.seq, rows.ctx
    m = xbc.shape[0]
    q = C_CHUNK
    nc_ctx = ctx // q
    nc_lat = seq // q
    nc = nc_ctx + nc_lat
    lat_chunks = rows.n_lat // q
    hp = C_HPG * C_HD

    def chunk(b, d, s):
        c_ctx = jnp.where(d == 0, s, nc_ctx - 1 - s)
        c_lat = jnp.where(d == 0, s - nc_ctx, nc - 1 - s)
        return jnp.where(s < nc_ctx, lat_chunks + b * nc_ctx + c_ctx, b * nc_lat + c_lat)

    n_xb = C_DI // C_STATE
    n_bb = C_GROUPS
    heads = 2 * C_GROUPS * C_HPG
    col = lambda t: t.astype(F32).reshape(heads, 1)
    dsk_e = jnp.repeat(d_skip.astype(F32).reshape(2 * C_GROUPS, 1, C_HPG), C_HD, axis=2)
    hspec = pl.BlockSpec((C_HPG, 1), lambda b, g, d, s: (d * C_GROUPS + g, 0))
    return pl.pallas_call(
        _ssd_kernel,
        grid=(batch, C_GROUPS, 2, nc),
        in_specs=[pl.BlockSpec((q, hp), lambda b, g, d, s: (chunk(b, d, s), g)),
                  pl.BlockSpec((q, C_STATE), lambda b, g, d, s: (chunk(b, d, s), n_xb + g)),
                  pl.BlockSpec((q, C_STATE), lambda b, g, d, s: (chunk(b, d, s), n_xb + n_bb + g)),
                  pl.BlockSpec((C_HPG, q), lambda b, g, d, s: (d * C_GROUPS + g, chunk(b, d, s))),
                  hspec, hspec,
                  pl.BlockSpec((None, 1, hp), lambda b, g, d, s: (d * C_GROUPS + g, 0, 0))],
        out_specs=pl.BlockSpec((None, q, hp), lambda b, g, d, s: (d, chunk(b, d, s), g)),
        out_shape=jax.ShapeDtypeStruct((2, m, C_DI), BF16),
        scratch_shapes=[pltpu.VMEM((C_STATE, hp), F32)],
        compiler_params=_params("parallel", "parallel", "parallel", "arbitrary"),
        name="ssd_scan",
    )(xbc, xbc, xbc, dt_t, col(a_log), col(dt_bias), dsk_e)


def _gate_norm_kernel(y_ref, z_ref, g_ref, o_ref):
    v = (y_ref[0].astype(F32) + y_ref[1].astype(F32)) * _silu(z_ref[...].astype(F32))
    o_ref[...] = (v * lax.rsqrt(jnp.mean(v * v, axis=1, keepdims=True) + EPS) * g_ref[...]).astype(o_ref.dtype)


def _gate_norm(y, z, norm_g, bm=256):
    _, m, n = y.shape
    return pl.pallas_call(
        _gate_norm_kernel,
        grid=(m // bm,),
        in_specs=[pl.BlockSpec((2, bm, n), lambda i: (0, i, 0)),
                  pl.BlockSpec((bm, n), lambda i: (i, 0)),
                  pl.BlockSpec((1, n), lambda i: (0, 0))],
        out_specs=pl.BlockSpec((bm, n), lambda i: (i, 0)),
        out_shape=jax.ShapeDtypeStruct((m, n), BF16),
        compiler_params=_params("parallel"),
        name="gate_norm",
    )(y, z, norm_g.astype(F32).reshape(1, n))


def _rope_tables(seq, dim, n_copies, bm):
    t = jnp.arange(seq, dtype=jnp.int32)
    row = (t // GRID_W).astype(F32)
    colp = (t % GRID_W).astype(F32)
    n_pairs = dim // 4
    inv = ROPE_BASE ** (-jnp.arange(n_pairs, dtype=F32) / n_pairs)
    ang = jnp.concatenate([row[:, None] * inv, colp[:, None] * inv], axis=-1)
    cos = jnp.tile(jnp.cos(ang), (1, 2 * n_copies))
    sin = jnp.tile(jnp.sin(ang), (1, n_copies))
    sin = jnp.concatenate([-sin, sin], axis=-1)
    cos = jnp.concatenate([cos, jnp.ones((bm, HEAD), F32)], axis=0)
    sin = jnp.concatenate([sin, jnp.zeros((bm, HEAD), F32)], axis=0)
    return cos, sin


def _deinterleave_perm(n_maps):
    dim = HEAD // n_maps
    perm = []
    for e in range(2):
        for mi in range(n_maps):
            for p in range(dim // 2):
                perm.append(mi * dim + 2 * p + e)
    return perm


def _head_perm(n_heads, n_maps):
    base = _deinterleave_perm(n_maps)
    return jnp.asarray([h * HEAD + c for h in range(n_heads) for c in base], dtype=jnp.int32)


def _mixer_a(u, w_in, lam_vec, sub_g, rows, lam_init, need_ctx):
    d = D_MODEL
    perm = _head_perm(N_HEADS, 2)
    w = jnp.concatenate([w_in[:, :d][:, perm], w_in[:, d:2 * d][:, perm], w_in[:, 2 * d:]], axis=1).astype(BF16)
    rope = _rope_tables(rows.seq, A_DK, 2, rows.bm)
    qh = _proj(u, w, 0, d, rows, mode="rope", rope=rope, scale=A_DK ** -0.5)
    kh = _proj(u, w, d, d, rows, mode="rope", rope=rope)
    vh = _proj(u, w, 2 * d, d, rows)
    kw = dict(kv_group=1, n_maps=2, lam_vec=lam_vec.astype(F32), sub_g=sub_g.astype(F32).reshape(1, HEAD),
              lam_init=lam_init)
    o_l = _attention(qh, kh, vh, rows, **kw)
    if not need_ctx:
        return o_l
    o_c = _attention(qh, kh, vh, rows, ctx_queries=True, **kw)
    return jnp.concatenate([o_l, o_c], axis=0)


def _mixer_b(u, w_in, qn_g, kn_g, rows, need_ctx):
    d = D_MODEL
    dkv = B_KV_HEADS * HEAD
    base = jnp.asarray(_deinterleave_perm(1), dtype=jnp.int32)
    w = jnp.concatenate([w_in[:, :d][:, _head_perm(N_HEADS, 1)], w_in[:, d:d + dkv][:, _head_perm(B_KV_HEADS, 1)],
                         w_in[:, d + dkv:]], axis=1).astype(BF16)
    rope = _rope_tables(rows.seq, HEAD, 1, rows.bm)
    qh = _proj(u, w, 0, d, rows, mode="rope", rope=rope, scale=HEAD ** -0.5,
               gain=qn_g.astype(F32)[base].reshape(1, HEAD))
    kh = _proj(u, w, d, dkv, rows, mode="rope", rope=rope, gain=kn_g.astype(F32)[base].reshape(1, HEAD), bn=dkv)
    vh = _proj(u, w, d + dkv, dkv, rows, bn=dkv)
    o_l = _attention(qh, kh, vh, rows, kv_group=B_GROUP)
    if not need_ctx:
        return o_l
    o_c = _attention(qh, kh, vh, rows, kv_group=B_GROUP, ctx_queries=True)
    return jnp.concatenate([o_l, o_c], axis=0)


def _mixer_c(u, w_in, conv_w, conv_b, a_log, dt_bias, d_skip, norm_g, rows):
    w = w_in.astype(BF16)
    n_dt = w_in.shape[1] - C_DI - C_CONV_DIM
    z = _proj(u, w, 0, C_DI, rows)
    xbc = _proj(u, w, C_DI, C_CONV_DIM, rows)
    dt = _proj(u, w, C_DI + C_CONV_DIM, n_dt, rows, out_dtype=F32, bn=n_dt)
    xbc = _conv_silu(xbc, conv_w.astype(F32), conv_b.astype(F32), rows)
    y = _ssd(xbc, dt.T, a_log, dt_bias, d_skip, rows)
    return _gate_norm(y, z, norm_g)


def _mixer_d(u, w_in, rpb, rows, need_ctx):
    d = D_MODEL
    w = w_in.astype(BF16)
    qh = _proj(u, w, 0, d, rows, scale=HEAD ** -0.5)
    kh = _proj(u, w, d, d, rows)
    vh = _proj(u, w, 2 * d, d, rows)
    o_l = _neighbourhood(qh, kh, vh, rpb, rows)
    if not need_ctx:
        return o_l
    o_c = _attention(qh, kh, vh, rows, kv_group=1, ctx_queries=True)
    return jnp.concatenate([o_l, o_c], axis=0)


def _row_block(seq, ctx_total, target=1024):
    bm = math.gcd(seq, ctx_total)
    while bm > target and bm % 2 == 0:
        bm //= 2
    return bm


def kernel(x, c, ctx, c_ctx, ada_w, ada_b, ln_g, ln_b, mlp_w1, mlp_w2, a_w_in, a_lambda, a_sub_g, a_w_out, b_w_in, b_qn_g, b_kn_g, b_w_out, c_w_in, c_conv_w, c_conv_b, c_A_log, c_dt_bias, c_D, c_norm_g, c_w_out, d_w_in, d_rpb, d_w_out):
    batch, seq, d = x.shape
    n_ctx = ctx.shape[1]
    depth = ada_w.shape[0]
    assert d == D_MODEL and batch < SUBLANES and seq % GRID_W == 0
    rows = _Rows(batch, seq, n_ctx, _row_block(seq, batch * n_ctx))

    cvec = jnp.zeros((SUBLANES, d), F32).at[:batch].set(c.astype(F32)).at[batch].set(c_ctx.astype(F32))
    mods = _ada_mods(cvec, ada_w.astype(F32), ada_b.astype(F32))
    x_all = jnp.concatenate([x.reshape(batch * seq, d), ctx.reshape(batch * n_ctx, d)], axis=0).astype(F32)
    u = _modulate(x_all, mods, 0, rows)

    for i in range(depth):
        mixer, j = i % N_MIXERS, i // N_MIXERS
        last = i == depth - 1
        if mixer == 0:
            o = _mixer_a(u, a_w_in[j], a_lambda[j], a_sub_g[j], rows, 0.8 - 0.6 * math.exp(-0.3 * i), not last)
            w_out = a_w_out[j]
        elif mixer == 1:
            o = _mixer_b(u, b_w_in[j], b_qn_g[j], b_kn_g[j], rows, not last)
            w_out = b_w_out[j]
        elif mixer == 2:
            o = _mixer_c(u, c_w_in[j], c_conv_w[j], c_conv_b[j], c_A_log[j], c_dt_bias[j], c_D[j], c_norm_g[j], rows)
            w_out = c_w_out[j]
        else:
            o = _mixer_d(u, d_w_in[j], d_rpb[j], rows, not last)
            w_out = d_w_out[j]
        n_rows = rows.n_lat if last else rows.n_all
        x_all, u = _mm_ln(o, w_out.astype(BF16), x_all, mods, i, 2, ln_g[i, 0].astype(F32), ln_b[i, 0].astype(F32),
                          rows, next_mod=(i, 3, 4), n_rows=n_rows)
        hidden = _proj(u, mlp_w1[i].astype(BF16), 0, D_FF, rows, mode="relu2", n_rows=n_rows)
        x_all, u = _mm_ln(hidden, mlp_w2[i].astype(BF16), x_all, mods, i, 5, ln_g[i, 1].astype(F32),
                          ln_b[i, 1].astype(F32), rows, next_mod=None if last else (i + 1, 0, 1), n_rows=n_rows)
    return x_all.reshape(batch, seq, d).astype(x.dtype)
```

```python
import functools
import math

import jax
import jax.numpy as jnp
from jax import lax
from jax.experimental import pallas as pl
from jax.experimental.pallas import tpu as pltpu

F32 = jnp.float32
BF16 = jnp.bfloat16

D_MODEL = 2048
DEPTH = 4
N_MIXERS = 4
GRID_W = 64
D_FF = 4 * D_MODEL
ROPE_BASE = 10000.0
EPS = 1e-6
DN_ALPHA = (2.0 * DEPTH) ** 0.25
N_HEADS = D_MODEL // 128
HEAD = 128
A_DK = 64
B_KV_HEADS = 4
B_GROUP = N_HEADS // B_KV_HEADS
C_DI = 2 * D_MODEL
C_HD = 64
C_GROUPS = 8
C_HPG = (C_DI // C_HD) // C_GROUPS
C_STATE = 128
C_CONV = 5
C_CHUNK = 128
C_CONV_DIM = C_DI + 2 * C_GROUPS * C_STATE
NA_ROWS = 8
NA_COLS = 16
NA_QROWS = 8
NA_WROWS = NA_QROWS + NA_ROWS
MASKED = -1e30
LOG2E = math.log2(math.e)

V7X_VMEM_BYTES = 64 * 1024 * 1024
VMEM_LIMIT = V7X_VMEM_BYTES - 8 * 1024 * 1024
LANES = 128
SUBLANES = 8


def _params(*sem):
    return pltpu.CompilerParams(dimension_semantics=sem, vmem_limit_bytes=VMEM_LIMIT)


def _dot(a, b):
    return jnp.dot(a, b, preferred_element_type=F32)


def _dot_nt(a, b, precision=None):
    return lax.dot_general(a, b, (((1,), (1,)), ((), ())), preferred_element_type=F32, precision=precision)


def _silu(x):
    return x / (1.0 + jnp.exp(-x))


def _ada_kernel(c_ref, w_ref, b_ref, o_ref):
    sc = _silu(c_ref[...]).astype(BF16)
    o_ref[...] = _dot(sc, w_ref[...].astype(BF16)) + b_ref[...]


def _ada_mods(cvec, ada_w, ada_b):
    depth, d, n = ada_w.shape
    bn = 1024
    out = pl.pallas_call(
        _ada_kernel,
        grid=(depth, n // bn),
        in_specs=[
            pl.BlockSpec((SUBLANES, d), lambda l, j: (0, 0)),
            pl.BlockSpec((None, d, bn), lambda l, j: (l, 0, j)),
            pl.BlockSpec((None, 1, bn), lambda l, j: (l, 0, j)),
        ],
        out_specs=pl.BlockSpec((None, SUBLANES, bn), lambda l, j: (l, 0, j)),
        out_shape=jax.ShapeDtypeStruct((depth, SUBLANES, n), F32),
        compiler_params=_params("parallel", "parallel"),
        name="ada_mods",
    )(cvec, ada_w, ada_b.reshape(depth, 1, n))
    return out.reshape(depth, SUBLANES, 6, 1, d)


class _Rows:
    def __init__(self, batch, seq, ctx, bm):
        self.batch, self.seq, self.ctx, self.bm = batch, seq, ctx, bm
        self.n_lat = batch * seq
        self.n_all = batch * (seq + ctx)
        assert seq % bm == 0 and (batch * ctx) % bm == 0

    def mod_row(self, i):
        return jnp.minimum((i * self.bm) // self.seq, self.batch)

    def mod_spec(self, mods, layer, k, grid_pos):
        d = mods.shape[-1]
        return pl.BlockSpec((None, None, None, 1, d),
                            lambda *g: (layer, self.mod_row(g[grid_pos]), k, 0, 0))


def _modulate_kernel(x_ref, sh_ref, sc_ref, u_ref):
    u_ref[...] = (x_ref[...] * (1.0 + sc_ref[...]) + sh_ref[...]).astype(u_ref.dtype)


def _modulate(x_all, mods, layer, rows):
    m, d = x_all.shape
    bm = rows.bm
    return pl.pallas_call(
        _modulate_kernel,
        grid=(m // bm,),
        in_specs=[pl.BlockSpec((bm, d), lambda i: (i, 0)),
                  rows.mod_spec(mods, layer, 0, 0), rows.mod_spec(mods, layer, 1, 0)],
        out_specs=pl.BlockSpec((bm, d), lambda i: (i, 0)),
        out_shape=jax.ShapeDtypeStruct((m, d), BF16),
        compiler_params=_params("parallel"),
        name="modulate",
    )(x_all, mods, mods)


def _proj_kernel(*refs, mode, scale, norm):
    if mode == "rope":
        if norm:
            u_ref, w_ref, cos_ref, sin_ref, g_ref, o_ref = refs
        else:
            u_ref, w_ref, cos_ref, sin_ref, o_ref = refs
    else:
        u_ref, w_ref, o_ref = refs
    acc = _dot(u_ref[...], w_ref[...])
    if mode == "relu2":
        r = jnp.maximum(acc, 0.0)
        o_ref[...] = (r * r).astype(o_ref.dtype)
    elif mode == "plain":
        if scale != 1.0:
            acc = acc * scale
        o_ref[...] = acc.astype(o_ref.dtype)
    else:
        cos = cos_ref[...]
        sin = sin_ref[...]
        for c in range(acc.shape[1] // HEAD):
            x = acc[:, c * HEAD:(c + 1) * HEAD]
            if norm:
                x = x * lax.rsqrt(jnp.mean(x * x, axis=1, keepdims=True) + EPS) * g_ref[...]
            x = x * cos + pltpu.roll(x, HEAD // 2, 1) * sin
            if scale != 1.0:
                x = x * scale
            o_ref[:, c * HEAD:(c + 1) * HEAD] = x.astype(o_ref.dtype)


def _proj(u, w, col0, ncols, rows, *, mode="plain", scale=1.0, rope=None, gain=None,
          out_dtype=BF16, bn=1024, n_rows=None):
    m = u.shape[0] if n_rows is None else n_rows
    k = u.shape[1]
    bm = rows.bm
    bn = min(bn, ncols)
    assert ncols % bn == 0 and col0 % bn == 0 and m % bm == 0
    cb0 = col0 // bn
    in_specs = [pl.BlockSpec((bm, k), lambda j, i: (i, 0)),
                pl.BlockSpec((k, bn), lambda j, i: (0, cb0 + j))]
    args = [u, w]
    if mode == "rope":
        cos, sin = rope
        n_seq = rows.seq // bm
        n_lat = rows.n_lat // bm

        def tab(j, i):
            return (jnp.where(i < n_lat, i % n_seq, n_seq), 0)

        in_specs += [pl.BlockSpec((bm, HEAD), tab), pl.BlockSpec((bm, HEAD), tab)]
        args += [cos, sin]
        if gain is not None:
            in_specs.append(pl.BlockSpec((1, HEAD), lambda j, i: (0, 0)))
            args.append(gain)
    return pl.pallas_call(
        functools.partial(_proj_kernel, mode=mode, scale=scale, norm=gain is not None),
        grid=(ncols // bn, m // bm),
        in_specs=in_specs,
        out_specs=pl.BlockSpec((bm, bn), lambda j, i: (i, j)),
        out_shape=jax.ShapeDtypeStruct((m, ncols), out_dtype),
        compiler_params=_params("parallel", "parallel"),
        name="proj_" + mode,
    )(*args)


def _mm_ln_kernel(*refs, nk, emit_u):
    if emit_u:
        a_ref, w_ref, x_ref, gate_ref, g_ref, b_ref, sh_ref, sc_ref, xo_ref, uo_ref = refs[:10]
        rest = refs[10:]
    else:
        a_ref, w_ref, x_ref, gate_ref, g_ref, b_ref, xo_ref = refs[:7]
        rest = refs[7:]

    def finish(y):
        v = DN_ALPHA * x_ref[...] + gate_ref[...] * y
        mu = jnp.mean(v, axis=1, keepdims=True)
        vc = v - mu
        var = jnp.mean(vc * vc, axis=1, keepdims=True)
        o = vc * lax.rsqrt(var + EPS) * g_ref[...] + b_ref[...]
        xo_ref[...] = o
        if emit_u:
            uo_ref[...] = (o * (1.0 + sc_ref[...]) + sh_ref[...]).astype(uo_ref.dtype)

    if nk == 1:
        finish(_dot(a_ref[...], w_ref[...]))
    else:
        acc_ref, = rest
        kk = pl.program_id(1)

        @pl.when(kk == 0)
        def _():
            acc_ref[...] = jnp.zeros_like(acc_ref)

        acc_ref[...] += _dot(a_ref[...], w_ref[...])

        @pl.when(kk == nk - 1)
        def _():
            finish(acc_ref[...])


def _mm_ln(a, w, x_res, mods, layer, gate_k, ln_g, ln_b, rows, *, next_mod=None, n_rows=None):
    m = a.shape[0] if n_rows is None else n_rows
    k = a.shape[1]
    d = w.shape[1]
    if k <= 2048:
        bm, bk = 256, k
    else:
        bm, bk = 512, 1024
    bm = min(bm, rows.bm)
    nk = k // bk
    sub = _Rows(rows.batch, rows.seq, rows.ctx, bm)
    emit_u = next_mod is not None
    vec = pl.BlockSpec((1, d), lambda i, kk: (0, 0))
    in_specs = [pl.BlockSpec((bm, bk), lambda i, kk: (i, kk)),
                pl.BlockSpec((bk, d), lambda i, kk: (kk, 0)),
                pl.BlockSpec((bm, d), lambda i, kk: (i, 0)),
                sub.mod_spec(mods, layer, gate_k, 0), vec, vec]
    args = [a, w, x_res, mods, ln_g.reshape(1, d), ln_b.reshape(1, d)]
    out_specs = [pl.BlockSpec((bm, d), lambda i, kk: (i, 0))]
    out_shape = [jax.ShapeDtypeStruct((m, d), F32)]
    if emit_u:
        nl, ksh, ksc = next_mod
        in_specs += [sub.mod_spec(mods, nl, ksh, 0), sub.mod_spec(mods, nl, ksc, 0)]
        args += [mods, mods]
        out_specs.append(pl.BlockSpec((bm, d), lambda i, kk: (i, 0)))
        out_shape.append(jax.ShapeDtypeStruct((m, d), BF16))
    res = pl.pallas_call(
        functools.partial(_mm_ln_kernel, nk=nk, emit_u=emit_u),
        grid=(m // bm, nk),
        in_specs=in_specs,
        out_specs=out_specs,
        out_shape=out_shape,
        scratch_shapes=[pltpu.VMEM((bm, d), F32)] if nk > 1 else [],
        compiler_params=_params("parallel", "arbitrary"),
        name="mm_ln",
    )(*args)
    return (res[0], res[1]) if emit_u else (res[0], None)


def _attn_kernel(*refs, n_maps, with_lat, lam_init, key_chunk, sub_rows):
    refs = list(refs)
    if n_maps == 2:
        lam_ref = refs.pop(0)
    q_ref, kc_ref, vc_ref = refs[:3]
    refs = refs[3:]
    if with_lat:
        kl_ref, vl_ref = refs[:2]
        refs = refs[2:]
    if n_maps == 2:
        g_ref = refs.pop(0)
    o_ref, s_ref = refs

    segments = [(kc_ref, vc_ref, 0, kc_ref.shape[0], 0)]
    if with_lat:
        n_ctx = kc_ref.shape[0]
        segments += [(kl_ref, vl_ref, c * key_chunk, key_chunk, n_ctx + c * key_chunk)
                     for c in range(kl_ref.shape[0] // key_chunk)]
    n_sub = q_ref.shape[0] // sub_rows
    items = []
    for r in range(n_sub):
        q = q_ref[r * sub_rows:(r + 1) * sub_rows, :]
        for mi in range(n_maps):
            if n_maps == 2:
                lane = lax.broadcasted_iota(jnp.int32, q.shape, 1)
                items.append(jnp.where((lane // 32) % 2 == mi, q, jnp.zeros_like(q)))
            else:
                items.append(q)

    def lane_groups(t):
        return [t[:, g * LANES:(g + 1) * LANES] for g in range(t.shape[1] // LANES)]

    def scores(i, seg, mxv):
        k_ref, _, off, n, col = seg
        s = _dot_nt(items[i], k_ref[off:off + n, :])
        s_ref[i % 2, :, col:col + n] = s
        for part in lane_groups(s):
            mxv = part if mxv is None else jnp.maximum(mxv, part)
        return mxv

    def values(i, seg, mx, lsv, acc):
        _, v_ref, off, n, col = seg
        p = jnp.exp2(s_ref[i % 2, :, col:col + n] - mx)
        for part in lane_groups(p):
            lsv = lsv + part
        return lsv, acc + _dot(p.astype(BF16), v_ref[off:off + n, :])

    zeros = jnp.zeros((sub_rows, LANES), F32)
    outs = []
    mxv = None
    for seg in segments:
        mxv = scores(0, seg, mxv)
    for i in range(len(items)):
        mx = jnp.max(mxv, axis=1, keepdims=True)
        lsv, acc, mxv = zeros, zeros, None
        for seg in segments:
            if i + 1 < len(items):
                mxv = scores(i + 1, seg, mxv)
            lsv, acc = values(i, seg, mx, lsv, acc)
        outs.append(acc / jnp.sum(lsv, axis=1, keepdims=True))

    if n_maps == 2:
        lv = lam_ref[...]
        lam = (jnp.exp(jnp.sum(lv[0:1] * lv[1:2], axis=1, keepdims=True))
               - jnp.exp(jnp.sum(lv[2:3] * lv[3:4], axis=1, keepdims=True)) + lam_init)
    for r in range(n_sub):
        if n_maps == 2:
            o = outs[2 * r] - lam * outs[2 * r + 1]
            o = o * lax.rsqrt(jnp.mean(o * o, axis=1, keepdims=True) + EPS) * g_ref[...] * (1.0 - lam_init)
        else:
            o = outs[r]
        o_ref[r * sub_rows:(r + 1) * sub_rows, :] = o.astype(o_ref.dtype)


def _attention(q, k, v, rows, *, kv_group, n_maps=1, lam_vec=None, sub_g=None, lam_init=0.0,
               ctx_queries=False, key_chunk=512, sub_rows=256, items_per_step=4):
    batch, seq, ctx = rows.batch, rows.seq, rows.ctx
    assert seq % min(key_chunk, seq) == 0
    bq = sub_rows * max(items_per_step // n_maps, 1)
    while seq % bq:
        bq //= 2
    n_heads = q.shape[1] // HEAD
    lat_blocks = rows.n_lat // ctx
    if ctx_queries:
        bq = ctx
        nq = 1
        q_map = lambda b, h, j: (lat_blocks + b, h)
        m_out = batch * ctx
    else:
        nq = seq // bq
        q_map = lambda b, h, j: (b * nq + j, h)
        m_out = rows.n_lat
    kc_map = lambda b, h, j: (lat_blocks + b, h // kv_group)
    in_specs, args = [], []
    if n_maps == 2:
        in_specs.append(pl.BlockSpec(lam_vec.shape, lambda b, h, j: (0, 0)))
        args.append(lam_vec)
    in_specs += [pl.BlockSpec((bq, HEAD), q_map),
                 pl.BlockSpec((ctx, HEAD), kc_map), pl.BlockSpec((ctx, HEAD), kc_map)]
    args += [q, k, v]
    if not ctx_queries:
        kl_map = lambda b, h, j: (b, h // kv_group)
        in_specs += [pl.BlockSpec((seq, HEAD), kl_map), pl.BlockSpec((seq, HEAD), kl_map)]
        args += [k, v]
    if n_maps == 2:
        in_specs.append(pl.BlockSpec((1, HEAD), lambda b, h, j: (0, 0)))
        args.append(sub_g)
    o_map = (lambda b, h, j: (b, h)) if ctx_queries else (lambda b, h, j: (b * nq + j, h))
    return pl.pallas_call(
        functools.partial(_attn_kernel, n_maps=n_maps, with_lat=not ctx_queries, lam_init=lam_init,
                          key_chunk=min(key_chunk, seq), sub_rows=min(sub_rows, bq)),
        grid=(batch, n_heads, nq),
        in_specs=in_specs,
        out_specs=pl.BlockSpec((bq, HEAD), o_map),
        out_shape=jax.ShapeDtypeStruct((m_out, n_heads * HEAD), BF16),
        scratch_shapes=[pltpu.VMEM((2, min(sub_rows, bq), ctx if ctx_queries else ctx + seq), F32)],
        compiler_params=_params("parallel", "parallel", "arbitrary"),
        name="attn_ctx" if ctx_queries else "attn_lat",
    )(*args)


def _na_window_start(kind, a):
    if kind == 0:
        return max(a - NA_ROWS // 2, 0), 0
    if kind == 1:
        return a, NA_ROWS // 2
    return min(a + NA_ROWS // 2, NA_WROWS - NA_ROWS), NA_WROWS - NA_QROWS


def _na_build_bias(tiles_ref, bias_ref, kind):
    w = GRID_W
    lane = lax.broadcasted_iota(jnp.int32, (w, 2 * w), 1)
    masked = jnp.full((w, 2 * w), MASKED, F32)
    for a in range(NA_QROWS):
        lo, off = _na_window_start(kind, a)
        for c in range(0, NA_WROWS, 2):
            ok0 = lo <= c < lo + NA_ROWS
            ok1 = lo <= c + 1 < lo + NA_ROWS
            if ok0 or ok1:
                i0 = c - a - off + NA_ROWS - 1
                assert 0 <= i0 <= 2 * NA_ROWS - 2
                tile = tiles_ref[i0]
                if not ok1:
                    tile = jnp.where(lane < w, tile, MASKED)
                elif not ok0:
                    tile = jnp.where(lane >= w, tile, MASKED)
            else:
                tile = masked
            bias_ref[kind, a * w:(a + 1) * w, c * w:(c + 2) * w] = tile


def _na_kernel(q_ref, k_ref, v_ref, kc_ref, vc_ref, tiles_ref, o_ref, bias_ref, *, n_rows):
    b = pl.program_id(1)
    j = pl.program_id(2)
    nb = n_rows // NA_QROWS
    wlen = NA_WROWS * GRID_W

    for kind, jj in ((0, 0), (1, 1), (2, nb - 1)):
        @pl.when(jnp.logical_and(b == 0, j == jj))
        def _():
            _na_build_bias(tiles_ref, bias_ref, kind)

    kind = jnp.where(j == 0, 0, jnp.where(j == nb - 1, 2, 1))
    start = pl.multiple_of(jnp.clip(NA_QROWS * j - NA_ROWS // 2, 0, n_rows - NA_WROWS) * GRID_W, 4 * GRID_W)
    q = q_ref[...]
    kw = k_ref[pl.ds(start, wlen), :]
    vw = v_ref[pl.ds(start, wlen), :]
    s_w = _dot_nt(q, kw) + bias_ref[kind]
    s_c = _dot_nt(q, kc_ref[...])
    mx = jnp.maximum(jnp.max(s_w, axis=1, keepdims=True), jnp.max(s_c, axis=1, keepdims=True))
    p_w = jnp.exp2(s_w - mx)
    p_c = jnp.exp2(s_c - mx)
    den = jnp.sum(p_w, axis=1, keepdims=True) + jnp.sum(p_c, axis=1, keepdims=True)
    o = _dot(p_w.astype(BF16), vw) + _dot(p_c.astype(BF16), vc_ref[...])
    o_ref[...] = (o / den).astype(o_ref.dtype)


def _na_tiles_kernel(rpb_ref, o_ref):
    n = o_ref.shape[1]
    shift = GRID_W.bit_length() - 1
    jj = lax.broadcasted_iota(jnp.int32, (LANES, n), 0)
    pos = lax.broadcasted_iota(jnp.int32, (LANES, n), 1)
    rel = (pos & (GRID_W - 1)) - (pos >> shift) + (NA_COLS - 1)
    onehot = (rel == jj).astype(F32)
    vals = jnp.dot(rpb_ref[...], onehot, preferred_element_type=F32, precision=lax.Precision.HIGHEST)
    pos1 = lax.broadcasted_iota(jnp.int32, (1, n), 1)
    qc = pos1 >> shift
    kc = pos1 & (GRID_W - 1)
    d = kc - jnp.clip(qc - NA_COLS // 2, 0, GRID_W - NA_COLS)
    inside = jnp.logical_and(d >= 0, d < NA_COLS)
    o_ref[...] = jnp.where(inside, vals * LOG2E, MASKED)


def _na_tiles(rpb):
    n_heads, n_ri, n_ci = rpb.shape
    assert n_ri == 2 * NA_ROWS - 1 and n_ci == 2 * NA_COLS - 1 and n_ci <= LANES and n_ri < 2 * SUBLANES
    padded = jnp.pad(rpb.astype(F32), ((0, 0), (0, 2 * SUBLANES - n_ri), (0, LANES - n_ci)))
    n = GRID_W * GRID_W
    flat = pl.pallas_call(
        _na_tiles_kernel,
        grid=(n_heads,),
        in_specs=[pl.BlockSpec((None, 2 * SUBLANES, LANES), lambda h: (h, 0, 0))],
        out_specs=pl.BlockSpec((None, 2 * SUBLANES, n), lambda h: (h, 0, 0)),
        out_shape=jax.ShapeDtypeStruct((n_heads, 2 * SUBLANES, n), F32),
        compiler_params=_params("parallel"),
        name="na_tiles",
    )(padded)
    t = flat.reshape(n_heads, 2 * SUBLANES, GRID_W, GRID_W)
    return jnp.concatenate([t[:, :n_ri], t[:, 1:n_ri + 1]], axis=-1)


def _neighbourhood(q, k, v, rpb, rows):
    batch, seq, ctx = rows.batch, rows.seq, rows.ctx
    n_rows = seq // GRID_W
    assert n_rows % NA_QROWS == 0 and n_rows >= NA_WROWS
    nb = n_rows // NA_QROWS
    bq = NA_QROWS * GRID_W
    wlen = NA_WROWS * GRID_W
    lat_blocks = rows.n_lat // ctx
    tiles = _na_tiles(rpb)
    return pl.pallas_call(
        functools.partial(_na_kernel, n_rows=n_rows),
        grid=(N_HEADS, batch, nb),
        in_specs=[pl.BlockSpec((bq, HEAD), lambda h, b, j: (b * nb + j, h)),
                  pl.BlockSpec((seq, HEAD), lambda h, b, j: (b, h)),
                  pl.BlockSpec((seq, HEAD), lambda h, b, j: (b, h)),
                  pl.BlockSpec((ctx, HEAD), lambda h, b, j: (lat_blocks + b, h)),
                  pl.BlockSpec((ctx, HEAD), lambda h, b, j: (lat_blocks + b, h)),
                  pl.BlockSpec((None,) + tiles.shape[1:], lambda h, b, j: (h, 0, 0, 0))],
        out_specs=pl.BlockSpec((bq, HEAD), lambda h, b, j: (b * nb + j, h)),
        out_shape=jax.ShapeDtypeStruct((rows.n_lat, N_HEADS * HEAD), BF16),
        scratch_shapes=[pltpu.VMEM((3, bq, wlen), F32)],
        compiler_params=_params("arbitrary", "arbitrary", "arbitrary"),
        name="na_attn",
    )(q, k, v, k, v, tiles)


def _conv_kernel(prev_ref, cur_ref, next_ref, w_ref, b_ref, o_ref, ext_ref, *, blocks_per_seq, n_lat_blocks):
    i = pl.program_id(1)
    bm = cur_ref.shape[0]
    pos = i % blocks_per_seq
    is_lat = i < n_lat_blocks
    first = jnp.logical_or(jnp.logical_not(is_lat), pos == 0)
    last = jnp.logical_or(jnp.logical_not(is_lat), pos == blocks_per_seq - 1)
    halo_p = prev_ref[...].astype(F32)
    halo_n = next_ref[...].astype(F32)
    ext_ref[0:SUBLANES, :] = jnp.where(first, jnp.zeros_like(halo_p), halo_p)
    ext_ref[SUBLANES:SUBLANES + bm, :] = cur_ref[...].astype(F32)
    ext_ref[SUBLANES + bm:2 * SUBLANES + bm, :] = jnp.where(last, jnp.zeros_like(halo_n), halo_n)
    w = w_ref[...]
    acc = jnp.zeros(o_ref.shape, F32) + b_ref[...]
    for t in range(C_CONV):
        off = SUBLANES - C_CONV // 2 + t
        acc = acc + ext_ref[off:off + bm, :] * w[t:t + 1, :]
    o_ref[...] = _silu(acc).astype(o_ref.dtype)


def _conv_silu(xbc, conv_w, conv_b, rows, bc=2048):
    m, n = xbc.shape
    bm = rows.ctx
    assert rows.seq % bm == 0 and bm % SUBLANES == 0
    hb = bm // SUBLANES
    n8 = m // SUBLANES
    return pl.pallas_call(
        functools.partial(_conv_kernel, blocks_per_seq=rows.seq // bm, n_lat_blocks=rows.n_lat // bm),
        grid=(n // bc, m // bm),
        in_specs=[pl.BlockSpec((SUBLANES, bc), lambda j, i: (jnp.maximum(i * hb - 1, 0), j)),
                  pl.BlockSpec((bm, bc), lambda j, i: (i, j)),
                  pl.BlockSpec((SUBLANES, bc), lambda j, i: (jnp.minimum((i + 1) * hb, n8 - 1), j)),
                  pl.BlockSpec((C_CONV, bc), lambda j, i: (0, j)),
                  pl.BlockSpec((1, bc), lambda j, i: (0, j))],
        out_specs=pl.BlockSpec((bm, bc), lambda j, i: (i, j)),
        out_shape=jax.ShapeDtypeStruct((m, n), BF16),
        scratch_shapes=[pltpu.VMEM((bm + 2 * SUBLANES, bc), F32)],
        compiler_params=_params("parallel", "arbitrary"),
        name="conv_silu",
    )(xbc, xbc, xbc, conv_w, conv_b.reshape(1, n))


def _ssd_chunk(x_ref, b_ref, c_ref, dt_ref, alog_ref, dtb_ref, dsk_ref, y_ref, h_ref, forward):
    q = C_CHUNK
    row = lax.broadcasted_iota(jnp.int32, (q, q), 0)
    col = lax.broadcasted_iota(jnp.int32, (q, q), 1)
    before = (col <= row) if forward else (col >= row)
    before_f = before.astype(F32)
    eye = (row == col).astype(F32)
    hi = lax.Precision.HIGHEST

    dt_row = dt_ref[...] + dtb_ref[...]
    dt_row = jnp.maximum(dt_row, 0.0) + jnp.log1p(jnp.exp(-jnp.abs(dt_row)))
    dta_row = dt_row * (-jnp.exp(alog_ref[...]))
    cs_row = _dot_nt(dta_row, before_f, hi)
    cs_col = _dot_nt(before_f, dta_row, hi)
    dt_col = _dot_nt(eye, dt_row, hi)
    end_col = cs_col[q - 1:q, :] if forward else cs_col[0:1, :]

    def expand(t):
        return jnp.concatenate([jnp.broadcast_to(t[:, h:h + 1], (t.shape[0], C_HD)) for h in range(C_HPG)], axis=1)

    cs_e = expand(cs_col)
    end_e = expand(end_col)
    x = x_ref[...].astype(F32)
    xdt = x * expand(dt_col)
    bc = b_ref[...]
    cc = c_ref[...]
    cb = _dot_nt(cc, bc)
    h_old = h_ref[...]
    y = _dot(cc, h_old.astype(BF16)) * jnp.exp(cs_e) + dsk_ref[...] * x
    parts = []
    for h in range(C_HPG):
        decay = jnp.exp(jnp.where(before, cs_col[:, h:h + 1] - cs_row[h:h + 1, :], MASKED))
        parts.append(_dot((cb * decay).astype(BF16), xdt[:, h * C_HD:(h + 1) * C_HD].astype(BF16)))
    y_ref[...] = (y + jnp.concatenate(parts, axis=1)).astype(y_ref.dtype)

    bct = bc.astype(F32).T.astype(BF16)
    h_ref[...] = h_old * jnp.exp(end_e) + _dot(bct, (xdt * jnp.exp(end_e - cs_e)).astype(BF16))


def _ssd_kernel(*refs):
    fwd_in, bwd_in = refs[:7], refs[7:14]
    y_fwd, y_bwd, h_ref = refs[14:]

    @pl.when(pl.program_id(2) == 0)
    def _():
        h_ref[...] = jnp.zeros_like(h_ref)

    _ssd_chunk(*fwd_in, y_fwd, h_ref.at[0], True)
    _ssd_chunk(*bwd_in, y_bwd, h_ref.at[1], False)


def _ssd(xbc, dt_t, a_log, dt_bias, d_skip, rows):
    batch, seq, ctx = rows.batch, rows.seq, rows.ctx
    m = xbc.shape[0]
    q = C_CHUNK
    nc_ctx = ctx // q
    nc_lat = seq // q
    nc = nc_ctx + nc_lat
    lat_chunks = rows.n_lat // q
    hp = C_HPG * C_HD

    def chunk(d):
        def index(b, s):
            c_ctx = s if d == 0 else nc_ctx - 1 - s
            c_lat = s - nc_ctx if d == 0 else nc - 1 - s
            return jnp.where(s < nc_ctx, lat_chunks + b * nc_ctx + c_ctx, b * nc_lat + c_lat)
        return index

    n_xb = C_DI // C_STATE
    n_bb = C_GROUPS
    heads = 2 * C_GROUPS * C_HPG
    col = lambda t: t.astype(F32).reshape(heads, 1)
    dsk_e = jnp.repeat(d_skip.astype(F32).reshape(2 * C_GROUPS, 1, C_HPG), C_HD, axis=2)
    in_specs, args, out_specs = [], [], []
    for d in range(2):
        ch = chunk(d)
        hspec = pl.BlockSpec((C_HPG, 1), lambda b, g, s, d=d: (d * C_GROUPS + g, 0))
        in_specs += [pl.BlockSpec((q, hp), lambda b, g, s, ch=ch: (ch(b, s), g)),
                     pl.BlockSpec((q, C_STATE), lambda b, g, s, ch=ch: (ch(b, s), n_xb + g)),
                     pl.BlockSpec((q, C_STATE), lambda b, g, s, ch=ch: (ch(b, s), n_xb + n_bb + g)),
                     pl.BlockSpec((C_HPG, q), lambda b, g, s, ch=ch, d=d: (d * C_GROUPS + g, ch(b, s))),
                     hspec, hspec,
                     pl.BlockSpec((None, 1, hp), lambda b, g, s, d=d: (d * C_GROUPS + g, 0, 0))]
        args += [xbc, xbc, xbc, dt_t, col(a_log), col(dt_bias), dsk_e]
        out_specs.append(pl.BlockSpec((q, hp), lambda b, g, s, ch=ch: (ch(b, s), g)))
    return pl.pallas_call(
        _ssd_kernel,
        grid=(batch, C_GROUPS, nc),
        in_specs=in_specs,
        out_specs=out_specs,
        out_shape=[jax.ShapeDtypeStruct((m, C_DI), BF16)] * 2,
        scratch_shapes=[pltpu.VMEM((2, C_STATE, hp), F32)],
        compiler_params=_params("parallel", "parallel", "arbitrary"),
        name="ssd_scan",
    )(*args)


def _gate_norm_kernel(yf_ref, yb_ref, z_ref, g_ref, o_ref):
    v = (yf_ref[...].astype(F32) + yb_ref[...].astype(F32)) * _silu(z_ref[...].astype(F32))
    o_ref[...] = (v * lax.rsqrt(jnp.mean(v * v, axis=1, keepdims=True) + EPS) * g_ref[...]).astype(o_ref.dtype)


def _gate_norm(y_fwd, y_bwd, z, norm_g, bm=256):
    m, n = z.shape
    blk = pl.BlockSpec((bm, n), lambda i: (i, 0))
    return pl.pallas_call(
        _gate_norm_kernel,
        grid=(m // bm,),
        in_specs=[blk, blk, blk, pl.BlockSpec((1, n), lambda i: (0, 0))],
        out_specs=blk,
        out_shape=jax.ShapeDtypeStruct((m, n), BF16),
        compiler_params=_params("parallel"),
        name="gate_norm",
    )(y_fwd, y_bwd, z, norm_g.astype(F32).reshape(1, n))


def _rope_tables(seq, dim, n_copies, bm):
    t = jnp.arange(seq, dtype=jnp.int32)
    row = (t // GRID_W).astype(F32)
    colp = (t % GRID_W).astype(F32)
    n_pairs = dim // 4
    inv = ROPE_BASE ** (-jnp.arange(n_pairs, dtype=F32) / n_pairs)
    ang = jnp.concatenate([row[:, None] * inv, colp[:, None] * inv], axis=-1)
    cos = jnp.tile(jnp.cos(ang), (1, 2 * n_copies))
    sin = jnp.tile(jnp.sin(ang), (1, n_copies))
    sin = jnp.concatenate([-sin, sin], axis=-1)
    cos = jnp.concatenate([cos, jnp.ones((bm, HEAD), F32)], axis=0)
    sin = jnp.concatenate([sin, jnp.zeros((bm, HEAD), F32)], axis=0)
    return cos, sin


def _deinterleave_perm(n_maps):
    dim = HEAD // n_maps
    perm = []
    for e in range(2):
        for mi in range(n_maps):
            for p in range(dim // 2):
                perm.append(mi * dim + 2 * p + e)
    return perm


def _head_perm(n_heads, n_maps):
    base = _deinterleave_perm(n_maps)
    return jnp.asarray([h * HEAD + c for h in range(n_heads) for c in base], dtype=jnp.int32)


def _mixer_a(u, w_in, lam_vec, sub_g, rows, lam_init, need_ctx):
    d = D_MODEL
    perm = _head_perm(N_HEADS, 2)
    w = jnp.concatenate([w_in[:, :d][:, perm], w_in[:, d:2 * d][:, perm], w_in[:, 2 * d:]], axis=1).astype(BF16)
    rope = _rope_tables(rows.seq, A_DK, 2, rows.bm)
    qh = _proj(u, w, 0, d, rows, mode="rope", rope=rope, scale=A_DK ** -0.5 * LOG2E)
    kh = _proj(u, w, d, d, rows, mode="rope", rope=rope)
    vh = _proj(u, w, 2 * d, d, rows)
    kw = dict(kv_group=1, n_maps=2, lam_vec=lam_vec.astype(F32), sub_g=sub_g.astype(F32).reshape(1, HEAD),
              lam_init=lam_init)
    o_l = _attention(qh, kh, vh, rows, **kw)
    if not need_ctx:
        return o_l
    o_c = _attention(qh, kh, vh, rows, ctx_queries=True, **kw)
    return jnp.concatenate([o_l, o_c], axis=0)


def _mixer_b(u, w_in, qn_g, kn_g, rows, need_ctx):
    d = D_MODEL
    dkv = B_KV_HEADS * HEAD
    base = jnp.asarray(_deinterleave_perm(1), dtype=jnp.int32)
    w = jnp.concatenate([w_in[:, :d][:, _head_perm(N_HEADS, 1)], w_in[:, d:d + dkv][:, _head_perm(B_KV_HEADS, 1)],
                         w_in[:, d + dkv:]], axis=1).astype(BF16)
    rope = _rope_tables(rows.seq, HEAD, 1, rows.bm)
    qh = _proj(u, w, 0, d, rows, mode="rope", rope=rope, scale=HEAD ** -0.5 * LOG2E,
               gain=qn_g.astype(F32)[base].reshape(1, HEAD))
    kh = _proj(u, w, d, dkv, rows, mode="rope", rope=rope, gain=kn_g.astype(F32)[base].reshape(1, HEAD), bn=dkv)
    vh = _proj(u, w, d + dkv, dkv, rows, bn=dkv)
    o_l = _attention(qh, kh, vh, rows, kv_group=B_GROUP)
    if not need_ctx:
        return o_l
    o_c = _attention(qh, kh, vh, rows, kv_group=B_GROUP, ctx_queries=True)
    return jnp.concatenate([o_l, o_c], axis=0)


def _mixer_c(u, w_in, conv_w, conv_b, a_log, dt_bias, d_skip, norm_g, rows):
    w = w_in.astype(BF16)
    n_dt = w_in.shape[1] - C_DI - C_CONV_DIM
    z = _proj(u, w, 0, C_DI, rows)
    xbc = _proj(u, w, C_DI, C_CONV_DIM, rows)
    dt = _proj(u, w, C_DI + C_CONV_DIM, n_dt, rows, out_dtype=F32, bn=n_dt)
    xbc = _conv_silu(xbc, conv_w.astype(F32), conv_b.astype(F32), rows)
    y_fwd, y_bwd = _ssd(xbc, dt.T, a_log, dt_bias, d_skip, rows)
    return _gate_norm(y_fwd, y_bwd, z, norm_g)


def _mixer_d(u, w_in, rpb, rows, need_ctx):
    d = D_MODEL
    w = w_in.astype(BF16)
    qh = _proj(u, w, 0, d, rows, scale=HEAD ** -0.5 * LOG2E)
    kh = _proj(u, w, d, d, rows)
    vh = _proj(u, w, 2 * d, d, rows)
    o_l = _neighbourhood(qh, kh, vh, rpb, rows)
    if not need_ctx:
        return o_l
    o_c = _attention(qh, kh, vh, rows, kv_group=1, ctx_queries=True)
    return jnp.concatenate([o_l, o_c], axis=0)


def _row_block(seq, ctx_total, target=1024):
    bm = math.gcd(seq, ctx_total)
    while bm > target and bm % 2 == 0:
        bm //= 2
    return bm


def kernel(x, c, ctx, c_ctx, ada_w, ada_b, ln_g, ln_b, mlp_w1, mlp_w2, a_w_in, a_lambda, a_sub_g, a_w_out, b_w_in, b_qn_g, b_kn_g, b_w_out, c_w_in, c_conv_w, c_conv_b, c_A_log, c_dt_bias, c_D, c_norm_g, c_w_out, d_w_in, d_rpb, d_w_out):
    batch, seq, d = x.shape
    n_ctx = ctx.shape[1]
    depth = ada_w.shape[0]
    assert d == D_MODEL and batch < SUBLANES and seq % GRID_W == 0
    rows = _Rows(batch, seq, n_ctx, _row_block(seq, batch * n_ctx))

    cvec = jnp.zeros((SUBLANES, d), F32).at[:batch].set(c.astype(F32)).at[batch].set(c_ctx.astype(F32))
    mods = _ada_mods(cvec, ada_w.astype(F32), ada_b.astype(F32))
    x_all = jnp.concatenate([x.reshape(batch * seq, d), ctx.reshape(batch * n_ctx, d)], axis=0).astype(F32)
    u = _modulate(x_all, mods, 0, rows)

    for i in range(depth):
        mixer, j = i % N_MIXERS, i // N_MIXERS
        last = i == depth - 1
        if mixer == 0:
            o = _mixer_a(u, a_w_in[j], a_lambda[j], a_sub_g[j], rows, 0.8 - 0.6 * math.exp(-0.3 * i), not last)
            w_out = a_w_out[j]
        elif mixer == 1:
            o = _mixer_b(u, b_w_in[j], b_qn_g[j], b_kn_g[j], rows, not last)
            w_out = b_w_out[j]
        elif mixer == 2:
            o = _mixer_c(u, c_w_in[j], c_conv_w[j], c_conv_b[j], c_A_log[j], c_dt_bias[j], c_D[j], c_norm_g[j], rows)
            w_out = c_w_out[j]
        else:
            o = _mixer_d(u, d_w_in[j], d_rpb[j], rows, not last)
            w_out = d_w_out[j]
        n_rows = rows.n_lat if last else rows.n_all
        x_all, u = _mm_ln(o, w_out.astype(BF16), x_all, mods, i, 2, ln_g[i, 0].astype(F32), ln_b[i, 0].astype(F32),
                          rows, next_mod=(i, 3, 4), n_rows=n_rows)
        hidden = _proj(u, mlp_w1[i].astype(BF16), 0, D_FF, rows, mode="relu2", n_rows=n_rows)
        x_all, u = _mm_ln(hidden, mlp_w2[i].astype(BF16), x_all, mods, i, 5, ln_g[i, 1].astype(F32),
                          ln_b[i, 1].astype(F32), rows, next_mod=None if last else (i + 1, 0, 1), n_rows=n_rows)
    return x_all.reshape(batch, seq, d).astype(x.dtype)
```

```python
import functools
import math

import jax
import jax.numpy as jnp
from jax import lax
from jax.experimental import pallas as pl
from jax.experimental.pallas import tpu as pltpu

F32 = jnp.float32
BF16 = jnp.bfloat16

D_MODEL = 2048
DEPTH = 4
N_MIXERS = 4
GRID_W = 64
D_FF = 4 * D_MODEL
ROPE_BASE = 10000.0
EPS = 1e-6
DN_ALPHA = (2.0 * DEPTH) ** 0.25
N_HEADS = D_MODEL // 128
HEAD = 128
A_DK = 64
B_KV_HEADS = 4
B_GROUP = N_HEADS // B_KV_HEADS
C_DI = 2 * D_MODEL
C_HD = 64
C_GROUPS = 8
C_HPG = (C_DI // C_HD) // C_GROUPS
C_STATE = 128
C_CONV = 5
C_CHUNK = 128
C_CONV_DIM = C_DI + 2 * C_GROUPS * C_STATE
NA_ROWS = 8
NA_COLS = 16
NA_QROWS = 8
NA_WROWS = NA_QROWS + NA_ROWS
MASKED = -1e30
LOG2E = math.log2(math.e)
ATTN_SUB_ROWS = 256
ATTN_KEY_CHUNK = 512

V7X_VMEM_BYTES = 64 * 1024 * 1024
VMEM_LIMIT = V7X_VMEM_BYTES - 8 * 1024 * 1024
LANES = 128
SUBLANES = 8


def _params(*sem):
    return pltpu.CompilerParams(dimension_semantics=sem, vmem_limit_bytes=VMEM_LIMIT)


def _dot(a, b):
    return jnp.dot(a, b, preferred_element_type=F32)


def _dot_nt(a, b, precision=None):
    return lax.dot_general(a, b, (((1,), (1,)), ((), ())), preferred_element_type=F32, precision=precision)


def _silu(x):
    return x / (1.0 + jnp.exp(-x))


def _ada_kernel(c_ref, w_ref, b_ref, o_ref):
    sc = _silu(c_ref[...]).astype(BF16)
    o_ref[...] = _dot(sc, w_ref[...].astype(BF16)) + b_ref[...]


def _ada_mods(cvec, ada_w, ada_b):
    depth, d, n = ada_w.shape
    bn = 1024
    out = pl.pallas_call(
        _ada_kernel,
        grid=(depth, n // bn),
        in_specs=[
            pl.BlockSpec((SUBLANES, d), lambda l, j: (0, 0)),
            pl.BlockSpec((None, d, bn), lambda l, j: (l, 0, j)),
            pl.BlockSpec((None, 1, bn), lambda l, j: (l, 0, j)),
        ],
        out_specs=pl.BlockSpec((None, SUBLANES, bn), lambda l, j: (l, 0, j)),
        out_shape=jax.ShapeDtypeStruct((depth, SUBLANES, n), F32),
        compiler_params=_params("parallel", "parallel"),
        name="ada_mods",
    )(cvec, ada_w, ada_b.reshape(depth, 1, n))
    return out.reshape(depth, SUBLANES, 6, 1, d)


class _Rows:
    def __init__(self, batch, seq, ctx, bm):
        self.batch, self.seq, self.ctx, self.bm = batch, seq, ctx, bm
        self.n_lat = batch * seq
        self.n_all = batch * (seq + ctx)
        assert seq % bm == 0 and (batch * ctx) % bm == 0

    def mod_row(self, i):
        return jnp.minimum((i * self.bm) // self.seq, self.batch)

    def mod_spec(self, mods, layer, k, grid_pos):
        d = mods.shape[-1]
        return pl.BlockSpec((None, None, None, 1, d),
                            lambda *g: (layer, self.mod_row(g[grid_pos]), k, 0, 0))


def _modulate_kernel(x_ref, sh_ref, sc_ref, u_ref):
    u_ref[...] = (x_ref[...] * (1.0 + sc_ref[...]) + sh_ref[...]).astype(u_ref.dtype)


def _modulate(x_all, mods, layer, rows):
    m, d = x_all.shape
    bm = rows.bm
    return pl.pallas_call(
        _modulate_kernel,
        grid=(m // bm,),
        in_specs=[pl.BlockSpec((bm, d), lambda i: (i, 0)),
                  rows.mod_spec(mods, layer, 0, 0), rows.mod_spec(mods, layer, 1, 0)],
        out_specs=pl.BlockSpec((bm, d), lambda i: (i, 0)),
        out_shape=jax.ShapeDtypeStruct((m, d), BF16),
        compiler_params=_params("parallel"),
        name="modulate",
    )(x_all, mods, mods)


def _proj_kernel(*refs, mode, scale, norm):
    if mode == "rope":
        if norm:
            u_ref, w_ref, cos_ref, sin_ref, g_ref, o_ref = refs
        else:
            u_ref, w_ref, cos_ref, sin_ref, o_ref = refs
    else:
        u_ref, w_ref, o_ref = refs
    acc = _dot(u_ref[...], w_ref[...])
    if mode == "relu2":
        r = jnp.maximum(acc, 0.0)
        o_ref[...] = (r * r).astype(o_ref.dtype)
    elif mode == "plain":
        if scale != 1.0:
            acc = acc * scale
        o_ref[...] = acc.astype(o_ref.dtype)
    else:
        cos = cos_ref[...]
        sin = sin_ref[...]
        for c in range(acc.shape[1] // HEAD):
            x = acc[:, c * HEAD:(c + 1) * HEAD]
            if norm:
                x = x * lax.rsqrt(jnp.mean(x * x, axis=1, keepdims=True) + EPS) * g_ref[...]
            x = x * cos + pltpu.roll(x, HEAD // 2, 1) * sin
            if scale != 1.0:
                x = x * scale
            o_ref[:, c * HEAD:(c + 1) * HEAD] = x.astype(o_ref.dtype)


def _proj(u, w, col0, ncols, rows, *, mode="plain", scale=1.0, rope=None, gain=None,
          out_dtype=BF16, bn=1024, n_rows=None):
    m = u.shape[0] if n_rows is None else n_rows
    k = u.shape[1]
    bm = rows.bm
    bn = min(bn, ncols)
    assert ncols % bn == 0 and col0 % bn == 0 and m % bm == 0
    cb0 = col0 // bn
    in_specs = [pl.BlockSpec((bm, k), lambda j, i: (i, 0)),
                pl.BlockSpec((k, bn), lambda j, i: (0, cb0 + j))]
    args = [u, w]
    if mode == "rope":
        cos, sin = rope
        n_seq = rows.seq // bm
        n_lat = rows.n_lat // bm

        def tab(j, i):
            return (jnp.where(i < n_lat, i % n_seq, n_seq), 0)

        in_specs += [pl.BlockSpec((bm, HEAD), tab), pl.BlockSpec((bm, HEAD), tab)]
        args += [cos, sin]
        if gain is not None:
            in_specs.append(pl.BlockSpec((1, HEAD), lambda j, i: (0, 0)))
            args.append(gain)
    return pl.pallas_call(
        functools.partial(_proj_kernel, mode=mode, scale=scale, norm=gain is not None),
        grid=(ncols // bn, m // bm),
        in_specs=in_specs,
        out_specs=pl.BlockSpec((bm, bn), lambda j, i: (i, j)),
        out_shape=jax.ShapeDtypeStruct((m, ncols), out_dtype),
        compiler_params=_params("parallel", "parallel"),
        name="proj_" + mode,
    )(*args)


def _mm_ln_kernel(*refs, nk, emit_u):
    if emit_u:
        a_ref, w_ref, x_ref, gate_ref, g_ref, b_ref, sh_ref, sc_ref, xo_ref, uo_ref = refs[:10]
        rest = refs[10:]
    else:
        a_ref, w_ref, x_ref, gate_ref, g_ref, b_ref, xo_ref = refs[:7]
        rest = refs[7:]

    def finish(y):
        v = DN_ALPHA * x_ref[...] + gate_ref[...] * y
        mu = jnp.mean(v, axis=1, keepdims=True)
        vc = v - mu
        var = jnp.mean(vc * vc, axis=1, keepdims=True)
        o = vc * lax.rsqrt(var + EPS) * g_ref[...] + b_ref[...]
        xo_ref[...] = o
        if emit_u:
            uo_ref[...] = (o * (1.0 + sc_ref[...]) + sh_ref[...]).astype(uo_ref.dtype)

    if nk == 1:
        finish(_dot(a_ref[...], w_ref[...]))
    else:
        acc_ref, = rest
        kk = pl.program_id(1)

        @pl.when(kk == 0)
        def _():
            acc_ref[...] = _dot(a_ref[...], w_ref[...])

        @pl.when(jnp.logical_and(kk > 0, kk < nk - 1))
        def _():
            acc_ref[...] = _dot(a_ref[...], w_ref[...]) + acc_ref[...]

        @pl.when(kk == nk - 1)
        def _():
            finish(_dot(a_ref[...], w_ref[...]) + acc_ref[...])


def _mm_ln(a, w, x_res, mods, layer, gate_k, ln_g, ln_b, rows, *, next_mod=None, n_rows=None):
    m = a.shape[0] if n_rows is None else n_rows
    k = a.shape[1]
    d = w.shape[1]
    if k <= 2048:
        bm, bk = 256, k
    else:
        bm, bk = 512, 2048
    bm = min(bm, rows.bm)
    nk = k // bk
    sub = _Rows(rows.batch, rows.seq, rows.ctx, bm)
    emit_u = next_mod is not None
    vec = pl.BlockSpec((1, d), lambda i, kk: (0, 0))
    in_specs = [pl.BlockSpec((bm, bk), lambda i, kk: (i, kk)),
                pl.BlockSpec((bk, d), lambda i, kk: (kk, 0)),
                pl.BlockSpec((bm, d), lambda i, kk: (i, 0)),
                sub.mod_spec(mods, layer, gate_k, 0), vec, vec]
    args = [a, w, x_res, mods, ln_g.reshape(1, d), ln_b.reshape(1, d)]
    out_specs = [pl.BlockSpec((bm, d), lambda i, kk: (i, 0))]
    out_shape = [jax.ShapeDtypeStruct((m, d), F32)]
    if emit_u:
        nl, ksh, ksc = next_mod
        in_specs += [sub.mod_spec(mods, nl, ksh, 0), sub.mod_spec(mods, nl, ksc, 0)]
        args += [mods, mods]
        out_specs.append(pl.BlockSpec((bm, d), lambda i, kk: (i, 0)))
        out_shape.append(jax.ShapeDtypeStruct((m, d), BF16))
    res = pl.pallas_call(
        functools.partial(_mm_ln_kernel, nk=nk, emit_u=emit_u),
        grid=(m // bm, nk),
        in_specs=in_specs,
        out_specs=out_specs,
        out_shape=out_shape,
        scratch_shapes=[pltpu.VMEM((bm, d), F32)] if nk > 1 else [],
        compiler_params=_params("parallel", "arbitrary"),
        name="mm_ln",
    )(*args)
    return (res[0], res[1]) if emit_u else (res[0], None)


def _softmax_pipeline(n_items, n_rows, segments, score, value, s_ref):
    def lane_groups(t):
        return [t[:, g * LANES:(g + 1) * LANES] for g in range(t.shape[1] // LANES)]

    def pass1(i, seg, mxv):
        s = score(i, seg)
        col = seg[-1]
        s_ref[i % 2, :, col:col + s.shape[1]] = s
        for part in lane_groups(s):
            mxv = part if mxv is None else jnp.maximum(mxv, part)
        return mxv

    def pass2(i, seg, mx, lsv, acc):
        v = value(seg)
        col = seg[-1]
        p = jnp.exp2(s_ref[i % 2, :, col:col + v.shape[0]] - mx)
        for part in lane_groups(p):
            lsv = lsv + part
        return lsv, acc + _dot(p.astype(BF16), v)

    zeros = jnp.zeros((n_rows, LANES), F32)
    outs = []
    mxv = None
    for seg in segments:
        mxv = pass1(0, seg, mxv)
    for i in range(n_items):
        mx = jnp.max(mxv, axis=1, keepdims=True)
        lsv, acc, mxv = zeros, zeros, None
        for seg in segments:
            if i + 1 < n_items:
                mxv = pass1(i + 1, seg, mxv)
            lsv, acc = pass2(i, seg, mx, lsv, acc)
        outs.append(acc / jnp.sum(lsv, axis=1, keepdims=True))
    return outs


def _attn_kernel(*refs, n_maps, with_lat, lam_init, key_chunk, sub_rows):
    refs = list(refs)
    if n_maps == 2:
        lam_ref = refs.pop(0)
    q_ref, kc_ref, vc_ref = refs[:3]
    refs = refs[3:]
    if with_lat:
        kl_ref, vl_ref = refs[:2]
        refs = refs[2:]
    if n_maps == 2:
        g_ref = refs.pop(0)
    o_ref, s_ref = refs

    segments = [(kc_ref, vc_ref, 0, kc_ref.shape[0], 0)]
    if with_lat:
        n_ctx = kc_ref.shape[0]
        segments += [(kl_ref, vl_ref, c * key_chunk, key_chunk, n_ctx + c * key_chunk)
                     for c in range(kl_ref.shape[0] // key_chunk)]
    n_sub = q_ref.shape[0] // sub_rows
    items = []
    for r in range(n_sub):
        q = q_ref[r * sub_rows:(r + 1) * sub_rows, :]
        for mi in range(n_maps):
            if n_maps == 2:
                lane = lax.broadcasted_iota(jnp.int32, q.shape, 1)
                items.append(jnp.where((lane // 32) % 2 == mi, q, jnp.zeros_like(q)))
            else:
                items.append(q)

    def score(i, seg):
        k_ref, _, off, n, _ = seg
        return _dot_nt(items[i], k_ref[off:off + n, :])

    def value(seg):
        _, v_ref, off, n, _ = seg
        return v_ref[off:off + n, :]

    outs = _softmax_pipeline(len(items), sub_rows, segments, score, value, s_ref)

    if n_maps == 2:
        lv = lam_ref[...]
        lam = (jnp.exp(jnp.sum(lv[0:1] * lv[1:2], axis=1, keepdims=True))
               - jnp.exp(jnp.sum(lv[2:3] * lv[3:4], axis=1, keepdims=True)) + lam_init)
    for r in range(n_sub):
        if n_maps == 2:
            o = outs[2 * r] - lam * outs[2 * r + 1]
            o = o * lax.rsqrt(jnp.mean(o * o, axis=1, keepdims=True) + EPS) * g_ref[...] * (1.0 - lam_init)
        else:
            o = outs[r]
        o_ref[r * sub_rows:(r + 1) * sub_rows, :] = o.astype(o_ref.dtype)


def _attention(q, k, v, rows, *, kv_group, n_maps=1, lam_vec=None, sub_g=None, lam_init=0.0,
               ctx_queries=False, key_chunk=ATTN_KEY_CHUNK, sub_rows=ATTN_SUB_ROWS, items_per_step=4):
    batch, seq, ctx = rows.batch, rows.seq, rows.ctx
    assert seq % min(key_chunk, seq) == 0
    bq = sub_rows * max(items_per_step // n_maps, 1)
    while seq % bq:
        bq //= 2
    n_heads = q.shape[1] // HEAD
    lat_blocks = rows.n_lat // ctx
    if ctx_queries:
        bq = ctx
        nq = 1
        q_map = lambda b, h, j: (lat_blocks + b, h)
        m_out = batch * ctx
    else:
        nq = seq // bq
        q_map = lambda b, h, j: (b * nq + j, h)
        m_out = rows.n_lat
    kc_map = lambda b, h, j: (lat_blocks + b, h // kv_group)
    in_specs, args = [], []
    if n_maps == 2:
        in_specs.append(pl.BlockSpec(lam_vec.shape, lambda b, h, j: (0, 0)))
        args.append(lam_vec)
    in_specs += [pl.BlockSpec((bq, HEAD), q_map),
                 pl.BlockSpec((ctx, HEAD), kc_map), pl.BlockSpec((ctx, HEAD), kc_map)]
    args += [q, k, v]
    if not ctx_queries:
        kl_map = lambda b, h, j: (b, h // kv_group)
        in_specs += [pl.BlockSpec((seq, HEAD), kl_map), pl.BlockSpec((seq, HEAD), kl_map)]
        args += [k, v]
    if n_maps == 2:
        in_specs.append(pl.BlockSpec((1, HEAD), lambda b, h, j: (0, 0)))
        args.append(sub_g)
    o_map = (lambda b, h, j: (b, h)) if ctx_queries else (lambda b, h, j: (b * nq + j, h))
    return pl.pallas_call(
        functools.partial(_attn_kernel, n_maps=n_maps, with_lat=not ctx_queries, lam_init=lam_init,
                          key_chunk=min(key_chunk, seq), sub_rows=min(sub_rows, bq)),
        grid=(batch, n_heads, nq),
        in_specs=in_specs,
        out_specs=pl.BlockSpec((bq, HEAD), o_map),
        out_shape=jax.ShapeDtypeStruct((m_out, n_heads * HEAD), BF16),
        scratch_shapes=[pltpu.VMEM((2, min(sub_rows, bq), ctx if ctx_queries else ctx + seq), F32)],
        compiler_params=_params("parallel", "parallel", "arbitrary"),
        name="attn_ctx" if ctx_queries else "attn_lat",
    )(*args)


def _na_window_start(kind, a):
    if kind == 0:
        return max(a - NA_ROWS // 2, 0), 0
    if kind == 1:
        return a, NA_ROWS // 2
    return min(a + NA_ROWS // 2, NA_WROWS - NA_ROWS), NA_WROWS - NA_QROWS


def _na_build_bias(tiles_ref, bias_ref, kind):
    w = GRID_W
    lane = lax.broadcasted_iota(jnp.int32, (w, 2 * w), 1)
    masked = jnp.full((w, 2 * w), MASKED, F32)
    for a in range(NA_QROWS):
        lo, off = _na_window_start(kind, a)
        for c in range(0, NA_WROWS, 2):
            ok0 = lo <= c < lo + NA_ROWS
            ok1 = lo <= c + 1 < lo + NA_ROWS
            if ok0 or ok1:
                i0 = c - a - off + NA_ROWS - 1
                assert 0 <= i0 <= 2 * NA_ROWS - 2
                tile = tiles_ref[i0]
                if not ok1:
                    tile = jnp.where(lane < w, tile, MASKED)
                elif not ok0:
                    tile = jnp.where(lane >= w, tile, MASKED)
            else:
                tile = masked
            bias_ref[kind, a * w:(a + 1) * w, c * w:(c + 2) * w] = tile


def _na_kernel(q_ref, k_ref, v_ref, kc_ref, vc_ref, tiles_ref, o_ref, bias_ref, s_ref, *, n_rows,
               key_chunk=ATTN_KEY_CHUNK, sub_rows=ATTN_SUB_ROWS):
    b = pl.program_id(1)
    j = pl.program_id(2)
    nb = n_rows // NA_QROWS
    wlen = NA_WROWS * GRID_W

    for kind, jj in ((0, 0), (1, 1), (2, nb - 1)):
        @pl.when(jnp.logical_and(b == 0, j == jj))
        def _():
            _na_build_bias(tiles_ref, bias_ref, kind)

    kind = jnp.where(j == 0, 0, jnp.where(j == nb - 1, 2, 1))
    start = pl.multiple_of(jnp.clip(NA_QROWS * j - NA_ROWS // 2, 0, n_rows - NA_WROWS) * GRID_W, 4 * GRID_W)
    n_ctx = kc_ref.shape[0]
    n_items = q_ref.shape[0] // sub_rows
    segments = [(False, 0, n_ctx, 0)]
    segments += [(True, c * key_chunk, key_chunk, n_ctx + c * key_chunk) for c in range(wlen // key_chunk)]

    def score(i, seg):
        window, off, n, _ = seg
        q = q_ref[i * sub_rows:(i + 1) * sub_rows, :]
        if not window:
            return _dot_nt(q, kc_ref[...])
        k = k_ref[pl.ds(start + off, n), :]
        return _dot_nt(q, k) + bias_ref[kind, i * sub_rows:(i + 1) * sub_rows, off:off + n]

    def value(seg):
        window, off, n, _ = seg
        return v_ref[pl.ds(start + off, n), :] if window else vc_ref[...]

    outs = _softmax_pipeline(n_items, sub_rows, segments, score, value, s_ref)
    for i, o in enumerate(outs):
        o_ref[i * sub_rows:(i + 1) * sub_rows, :] = o.astype(o_ref.dtype)


def _na_tiles_kernel(rpb_ref, o_ref):
    n = o_ref.shape[1]
    shift = GRID_W.bit_length() - 1
    jj = lax.broadcasted_iota(jnp.int32, (LANES, n), 0)
    pos = lax.broadcasted_iota(jnp.int32, (LANES, n), 1)
    rel = (pos & (GRID_W - 1)) - (pos >> shift) + (NA_COLS - 1)
    onehot = (rel == jj).astype(F32)
    vals = jnp.dot(rpb_ref[...], onehot, preferred_element_type=F32, precision=lax.Precision.HIGHEST)
    pos1 = lax.broadcasted_iota(jnp.int32, (1, n), 1)
    qc = pos1 >> shift
    kc = pos1 & (GRID_W - 1)
    d = kc - jnp.clip(qc - NA_COLS // 2, 0, GRID_W - NA_COLS)
    inside = jnp.logical_and(d >= 0, d < NA_COLS)
    o_ref[...] = jnp.where(inside, vals * LOG2E, MASKED)


def _na_tiles(rpb):
    n_heads, n_ri, n_ci = rpb.shape
    assert n_ri == 2 * NA_ROWS - 1 and n_ci == 2 * NA_COLS - 1 and n_ci <= LANES and n_ri < 2 * SUBLANES
    padded = jnp.pad(rpb.astype(F32), ((0, 0), (0, 2 * SUBLANES - n_ri), (0, LANES - n_ci)))
    n = GRID_W * GRID_W
    flat = pl.pallas_call(
        _na_tiles_kernel,
        grid=(n_heads,),
        in_specs=[pl.BlockSpec((None, 2 * SUBLANES, LANES), lambda h: (h, 0, 0))],
        out_specs=pl.BlockSpec((None, 2 * SUBLANES, n), lambda h: (h, 0, 0)),
        out_shape=jax.ShapeDtypeStruct((n_heads, 2 * SUBLANES, n), F32),
        compiler_params=_params("parallel"),
        name="na_tiles",
    )(padded)
    t = flat.reshape(n_heads, 2 * SUBLANES, GRID_W, GRID_W)
    return jnp.concatenate([t[:, :n_ri], t[:, 1:n_ri + 1]], axis=-1)


def _neighbourhood(q, k, v, rpb, rows):
    batch, seq, ctx = rows.batch, rows.seq, rows.ctx
    n_rows = seq // GRID_W
    assert n_rows % NA_QROWS == 0 and n_rows >= NA_WROWS
    nb = n_rows // NA_QROWS
    bq = NA_QROWS * GRID_W
    wlen = NA_WROWS * GRID_W
    lat_blocks = rows.n_lat // ctx
    tiles = _na_tiles(rpb)
    return pl.pallas_call(
        functools.partial(_na_kernel, n_rows=n_rows),
        grid=(N_HEADS, batch, nb),
        in_specs=[pl.BlockSpec((bq, HEAD), lambda h, b, j: (b * nb + j, h)),
                  pl.BlockSpec((seq, HEAD), lambda h, b, j: (b, h)),
                  pl.BlockSpec((seq, HEAD), lambda h, b, j: (b, h)),
                  pl.BlockSpec((ctx, HEAD), lambda h, b, j: (lat_blocks + b, h)),
                  pl.BlockSpec((ctx, HEAD), lambda h, b, j: (lat_blocks + b, h)),
                  pl.BlockSpec((None,) + tiles.shape[1:], lambda h, b, j: (h, 0, 0, 0))],
        out_specs=pl.BlockSpec((bq, HEAD), lambda h, b, j: (b * nb + j, h)),
        out_shape=jax.ShapeDtypeStruct((rows.n_lat, N_HEADS * HEAD), BF16),
        scratch_shapes=[pltpu.VMEM((3, bq, wlen), F32), pltpu.VMEM((2, ATTN_SUB_ROWS, ctx + wlen), F32)],
        compiler_params=_params("arbitrary", "arbitrary", "arbitrary"),
        name="na_attn",
    )(q, k, v, k, v, tiles)


def _conv_kernel(prev_ref, cur_ref, next_ref, w_ref, b_ref, o_ref, ext_ref, *, blocks_per_seq, n_lat_blocks):
    i = pl.program_id(1)
    bm = cur_ref.shape[0]
    pos = i % blocks_per_seq
    is_lat = i < n_lat_blocks
    first = jnp.logical_or(jnp.logical_not(is_lat), pos == 0)
    last = jnp.logical_or(jnp.logical_not(is_lat), pos == blocks_per_seq - 1)
    halo_p = prev_ref[...].astype(F32)
    halo_n = next_ref[...].astype(F32)
    ext_ref[0:SUBLANES, :] = jnp.where(first, jnp.zeros_like(halo_p), halo_p)
    ext_ref[SUBLANES:SUBLANES + bm, :] = cur_ref[...].astype(F32)
    ext_ref[SUBLANES + bm:2 * SUBLANES + bm, :] = jnp.where(last, jnp.zeros_like(halo_n), halo_n)
    w = w_ref[...]
    acc = jnp.zeros(o_ref.shape, F32) + b_ref[...]
    for t in range(C_CONV):
        off = SUBLANES - C_CONV // 2 + t
        acc = acc + ext_ref[off:off + bm, :] * w[t:t + 1, :]
    o_ref[...] = _silu(acc).astype(o_ref.dtype)


def _conv_silu(xbc, conv_w, conv_b, rows, bc=2048):
    m, n = xbc.shape
    bm = rows.ctx
    assert rows.seq % bm == 0 and bm % SUBLANES == 0
    hb = bm // SUBLANES
    n8 = m // SUBLANES
    return pl.pallas_call(
        functools.partial(_conv_kernel, blocks_per_seq=rows.seq // bm, n_lat_blocks=rows.n_lat // bm),
        grid=(n // bc, m // bm),
        in_specs=[pl.BlockSpec((SUBLANES, bc), lambda j, i: (jnp.maximum(i * hb - 1, 0), j)),
                  pl.BlockSpec((bm, bc), lambda j, i: (i, j)),
                  pl.BlockSpec((SUBLANES, bc), lambda j, i: (jnp.minimum((i + 1) * hb, n8 - 1), j)),
                  pl.BlockSpec((C_CONV, bc), lambda j, i: (0, j)),
                  pl.BlockSpec((1, bc), lambda j, i: (0, j))],
        out_specs=pl.BlockSpec((bm, bc), lambda j, i: (i, j)),
        out_shape=jax.ShapeDtypeStruct((m, n), BF16),
        scratch_shapes=[pltpu.VMEM((bm + 2 * SUBLANES, bc), F32)],
        compiler_params=_params("parallel", "arbitrary"),
        name="conv_silu",
    )(xbc, xbc, xbc, conv_w, conv_b.reshape(1, n))


def _ssd_chunk(x_ref, b_ref, c_ref, dt_ref, alog_ref, dtb_ref, dsk_ref, y_ref, h_ref, forward):
    q = C_CHUNK
    half = q // 2
    assert q == 2 * C_HD and C_HPG % 2 == 0
    row = lax.broadcasted_iota(jnp.int32, (q, q), 0)
    col = lax.broadcasted_iota(jnp.int32, (q, q), 1)
    before = (col <= row) if forward else (col >= row)

    def thrice(mask):
        m = jnp.where(mask, 1.0, 0.0).astype(BF16)
        return jnp.concatenate([m, m, m], axis=1)

    def pieces(v):
        hi = v.astype(BF16).astype(F32)
        r = v - hi
        mid = r.astype(BF16).astype(F32)
        return hi, mid, r - mid

    def per_channel(v):
        rep = lambda t: jnp.concatenate([jnp.broadcast_to(t[h:h + 1, :], (C_HD, q)) for h in range(C_HPG)], axis=0)
        return jnp.concatenate([rep(t).astype(BF16) for t in pieces(v)], axis=1)

    dt_row = dt_ref[...] + dtb_ref[...]
    dt_row = jnp.maximum(dt_row, 0.0) + jnp.log1p(jnp.exp(-jnp.abs(dt_row)))
    dta_row = dt_row * (-jnp.exp(alog_ref[...]))
    before3 = thrice(before)
    cs_e = _dot_nt(before3, per_channel(dta_row))
    dt_e = _dot_nt(thrice(row == col), per_channel(dt_row))
    pad = jnp.zeros_like(dta_row)
    dta16 = jnp.concatenate([jnp.concatenate([t, pad], axis=0).astype(BF16) for t in pieces(dta_row)], axis=1)
    cs_row = _dot_nt(dta16, before3)[:C_HPG]
    end_e = cs_e[q - 1:q, :] if forward else cs_e[0:1, :]

    x = x_ref[...].astype(F32)
    xdt = x * dt_e
    bc = b_ref[...]
    cc = c_ref[...]
    cb = _dot_nt(cc, bc)
    h_old = h_ref[...]
    y = _dot(cc, h_old.astype(BF16)) * jnp.exp(cs_e) + dsk_ref[...] * x

    lane = lax.broadcasted_iota(jnp.int32, (1, q), 1)
    same_half = (row < half) == (col < half)
    parts = []
    for h0 in range(0, C_HPG, 2):
        cp = cs_e[:, h0 * C_HD:(h0 + 2) * C_HD]
        r1 = jnp.where(lane < half, cs_row[h0:h0 + 1, :], cs_row[h0 + 1:h0 + 2, :])
        r2 = jnp.where(lane < half, cs_row[h0 + 1:h0 + 2, :], cs_row[h0:h0 + 1, :])
        u1 = cb * jnp.exp(jnp.where(before, cp - r1, MASKED))
        u2 = cb * jnp.exp(jnp.where(before, pltpu.roll(cp, half, 1) - r2, MASKED))
        xp = xdt[:, h0 * C_HD:(h0 + 2) * C_HD]
        rhs = jnp.concatenate([jnp.where(same_half, xp, 0.0), jnp.where(same_half, 0.0, xp)], axis=0)
        parts.append(_dot(jnp.concatenate([u1, u2], axis=1).astype(BF16), rhs.astype(BF16)))
    y_ref[...] = (y + jnp.concatenate(parts, axis=1)).astype(y_ref.dtype)

    bct = bc.astype(F32).T.astype(BF16)
    h_ref[...] = h_old * jnp.exp(end_e) + _dot(bct, (xdt * jnp.exp(end_e - cs_e)).astype(BF16))


def _ssd_kernel(*refs):
    fwd_in, bwd_in = refs[:7], refs[7:14]
    y_fwd, y_bwd, h_ref = refs[14:]

    @pl.when(pl.program_id(2) == 0)
    def _():
        h_ref[...] = jnp.zeros_like(h_ref)

    _ssd_chunk(*fwd_in, y_fwd, h_ref.at[0], True)
    _ssd_chunk(*bwd_in, y_bwd, h_ref.at[1], False)


def _ssd(xbc, dt_t, a_log, dt_bias, d_skip, rows):
    batch, seq, ctx = rows.batch, rows.seq, rows.ctx
    m = xbc.shape[0]
    q = C_CHUNK
    nc_ctx = ctx // q
    nc_lat = seq // q
    nc = nc_ctx + nc_lat
    lat_chunks = rows.n_lat // q
    hp = C_HPG * C_HD

    def chunk(d):
        def index(b, s):
            c_ctx = s if d == 0 else nc_ctx - 1 - s
            c_lat = s - nc_ctx if d == 0 else nc - 1 - s
            return jnp.where(s < nc_ctx, lat_chunks + b * nc_ctx + c_ctx, b * nc_lat + c_lat)
        return index

    n_xb = C_DI // C_STATE
    n_bb = C_GROUPS
    heads = 2 * C_GROUPS * C_HPG
    col = lambda t: t.astype(F32).reshape(heads, 1)
    dsk_e = jnp.repeat(d_skip.astype(F32).reshape(2 * C_GROUPS, 1, C_HPG), C_HD, axis=2)
    in_specs, args, out_specs = [], [], []
    for d in range(2):
        ch = chunk(d)
        hspec = pl.BlockSpec((C_HPG, 1), lambda b, g, s, d=d: (d * C_GROUPS + g, 0))
        in_specs += [pl.BlockSpec((q, hp), lambda b, g, s, ch=ch: (ch(b, s), g)),
                     pl.BlockSpec((q, C_STATE), lambda b, g, s, ch=ch: (ch(b, s), n_xb + g)),
                     pl.BlockSpec((q, C_STATE), lambda b, g, s, ch=ch: (ch(b, s), n_xb + n_bb + g)),
                     pl.BlockSpec((C_HPG, q), lambda b, g, s, ch=ch, d=d: (d * C_GROUPS + g, ch(b, s))),
                     hspec, hspec,
                     pl.BlockSpec((None, 1, hp), lambda b, g, s, d=d: (d * C_GROUPS + g, 0, 0))]
        args += [xbc, xbc, xbc, dt_t, col(a_log), col(dt_bias), dsk_e]
        out_specs.append(pl.BlockSpec((q, hp), lambda b, g, s, ch=ch: (ch(b, s), g)))
    return pl.pallas_call(
        _ssd_kernel,
        grid=(batch, C_GROUPS, nc),
        in_specs=in_specs,
        out_specs=out_specs,
        out_shape=[jax.ShapeDtypeStruct((m, C_DI), BF16)] * 2,
        scratch_shapes=[pltpu.VMEM((2, C_STATE, hp), F32)],
        compiler_params=_params("parallel", "parallel", "arbitrary"),
        name="ssd_scan",
    )(*args)


def _gate_norm_kernel(yf_ref, yb_ref, z_ref, g_ref, o_ref):
    v = (yf_ref[...].astype(F32) + yb_ref[...].astype(F32)) * _silu(z_ref[...].astype(F32))
    o_ref[...] = (v * lax.rsqrt(jnp.mean(v * v, axis=1, keepdims=True) + EPS) * g_ref[...]).astype(o_ref.dtype)


def _gate_norm(y_fwd, y_bwd, z, norm_g, bm=256):
    m, n = z.shape
    blk = pl.BlockSpec((bm, n), lambda i: (i, 0))
    return pl.pallas_call(
        _gate_norm_kernel,
        grid=(m // bm,),
        in_specs=[blk, blk, blk, pl.BlockSpec((1, n), lambda i: (0, 0))],
        out_specs=blk,
        out_shape=jax.ShapeDtypeStruct((m, n), BF16),
        compiler_params=_params("parallel"),
        name="gate_norm",
    )(y_fwd, y_bwd, z, norm_g.astype(F32).reshape(1, n))


def _rope_tables(seq, dim, n_copies, bm):
    t = jnp.arange(seq, dtype=jnp.int32)
    row = (t // GRID_W).astype(F32)
    colp = (t % GRID_W).astype(F32)
    n_pairs = dim // 4
    inv = ROPE_BASE ** (-jnp.arange(n_pairs, dtype=F32) / n_pairs)
    ang = jnp.concatenate([row[:, None] * inv, colp[:, None] * inv], axis=-1)
    cos = jnp.tile(jnp.cos(ang), (1, 2 * n_copies))
    sin = jnp.tile(jnp.sin(ang), (1, n_copies))
    sin = jnp.concatenate([-sin, sin], axis=-1)
    cos = jnp.concatenate([cos, jnp.ones((bm, HEAD), F32)], axis=0)
    sin = jnp.concatenate([sin, jnp.zeros((bm, HEAD), F32)], axis=0)
    return cos, sin


def _deinterleave_perm(n_maps):
    dim = HEAD // n_maps
    perm = []
    for e in range(2):
        for mi in range(n_maps):
            for p in range(dim // 2):
                perm.append(mi * dim + 2 * p + e)
    return perm


def _head_perm(n_heads, n_maps):
    base = _deinterleave_perm(n_maps)
    return jnp.asarray([h * HEAD + c for h in range(n_heads) for c in base], dtype=jnp.int32)


def _mixer_a(u, w_in, lam_vec, sub_g, rows, lam_init, need_ctx):
    d = D_MODEL
    perm = _head_perm(N_HEADS, 2)
    w = jnp.concatenate([w_in[:, :d][:, perm], w_in[:, d:2 * d][:, perm], w_in[:, 2 * d:]], axis=1).astype(BF16)
    rope = _rope_tables(rows.seq, A_DK, 2, rows.bm)
    qh = _proj(u, w, 0, d, rows, mode="rope", rope=rope, scale=A_DK ** -0.5 * LOG2E)
    kh = _proj(u, w, d, d, rows, mode="rope", rope=rope)
    vh = _proj(u, w, 2 * d, d, rows)
    kw = dict(kv_group=1, n_maps=2, lam_vec=lam_vec.astype(F32), sub_g=sub_g.astype(F32).reshape(1, HEAD),
              lam_init=lam_init)
    o_l = _attention(qh, kh, vh, rows, **kw)
    if not need_ctx:
        return o_l
    o_c = _attention(qh, kh, vh, rows, ctx_queries=True, **kw)
    return jnp.concatenate([o_l, o_c], axis=0)


def _mixer_b(u, w_in, qn_g, kn_g, rows, need_ctx):
    d = D_MODEL
    dkv = B_KV_HEADS * HEAD
    base = jnp.asarray(_deinterleave_perm(1), dtype=jnp.int32)
    w = jnp.concatenate([w_in[:, :d][:, _head_perm(N_HEADS, 1)], w_in[:, d:d + dkv][:, _head_perm(B_KV_HEADS, 1)],
                         w_in[:, d + dkv:]], axis=1).astype(BF16)
    rope = _rope_tables(rows.seq, HEAD, 1, rows.bm)
    qh = _proj(u, w, 0, d, rows, mode="rope", rope=rope, scale=HEAD ** -0.5 * LOG2E,
               gain=qn_g.astype(F32)[base].reshape(1, HEAD))
    kh = _proj(u, w, d, dkv, rows, mode="rope", rope=rope, gain=kn_g.astype(F32)[base].reshape(1, HEAD), bn=dkv)
    vh = _proj(u, w, d + dkv, dkv, rows, bn=dkv)
    o_l = _attention(qh, kh, vh, rows, kv_group=B_GROUP)
    if not need_ctx:
        return o_l
    o_c = _attention(qh, kh, vh, rows, kv_group=B_GROUP, ctx_queries=True)
    return jnp.concatenate([o_l, o_c], axis=0)


def _mixer_c(u, w_in, conv_w, conv_b, a_log, dt_bias, d_skip, norm_g, rows):
    w = w_in.astype(BF16)
    n_dt = w_in.shape[1] - C_DI - C_CONV_DIM
    z = _proj(u, w, 0, C_DI, rows)
    xbc = _proj(u, w, C_DI, C_CONV_DIM, rows)
    dt = _proj(u, w, C_DI + C_CONV_DIM, n_dt, rows, out_dtype=F32, bn=n_dt)
    xbc = _conv_silu(xbc, conv_w.astype(F32), conv_b.astype(F32), rows)
    y_fwd, y_bwd = _ssd(xbc, dt.T, a_log, dt_bias, d_skip, rows)
    return _gate_norm(y_fwd, y_bwd, z, norm_g)


def _mixer_d(u, w_in, rpb, rows, need_ctx):
    d = D_MODEL
    w = w_in.astype(BF16)
    qh = _proj(u, w, 0, d, rows, scale=HEAD ** -0.5 * LOG2E)
    kh = _proj(u, w, d, d, rows)
    vh = _proj(u, w, 2 * d, d, rows)
    o_l = _neighbourhood(qh, kh, vh, rpb, rows)
    if not need_ctx:
        return o_l
    o_c = _attention(qh, kh, vh, rows, kv_group=1, ctx_queries=True)
    return jnp.concatenate([o_l, o_c], axis=0)


def _row_block(seq, ctx_total, target=1024):
    bm = math.gcd(seq, ctx_total)
    while bm > target and bm % 2 == 0:
        bm //= 2
    return bm


def kernel(x, c, ctx, c_ctx, ada_w, ada_b, ln_g, ln_b, mlp_w1, mlp_w2, a_w_in, a_lambda, a_sub_g, a_w_out, b_w_in, b_qn_g, b_kn_g, b_w_out, c_w_in, c_conv_w, c_conv_b, c_A_log, c_dt_bias, c_D, c_norm_g, c_w_out, d_w_in, d_rpb, d_w_out):
    batch, seq, d = x.shape
    n_ctx = ctx.shape[1]
    depth = ada_w.shape[0]
    assert d == D_MODEL and batch < SUBLANES and seq % GRID_W == 0
    rows = _Rows(batch, seq, n_ctx, _row_block(seq, batch * n_ctx))

    cvec = jnp.zeros((SUBLANES, d), F32).at[:batch].set(c.astype(F32)).at[batch].set(c_ctx.astype(F32))
    mods = _ada_mods(cvec, ada_w.astype(F32), ada_b.astype(F32))
    x_all = jnp.concatenate([x.reshape(batch * seq, d), ctx.reshape(batch * n_ctx, d)], axis=0).astype(F32)
    u = _modulate(x_all, mods, 0, rows)

    for i in range(depth):
        mixer, j = i % N_MIXERS, i // N_MIXERS
        last = i == depth - 1
        if mixer == 0:
            o = _mixer_a(u, a_w_in[j], a_lambda[j], a_sub_g[j], rows, 0.8 - 0.6 * math.exp(-0.3 * i), not last)
            w_out = a_w_out[j]
        elif mixer == 1:
            o = _mixer_b(u, b_w_in[j], b_qn_g[j], b_kn_g[j], rows, not last)
            w_out = b_w_out[j]
        elif mixer == 2:
            o = _mixer_c(u, c_w_in[j], c_conv_w[j], c_conv_b[j], c_A_log[j], c_dt_bias[j], c_D[j], c_norm_g[j], rows)
            w_out = c_w_out[j]
        else:
            o = _mixer_d(u, d_w_in[j], d_rpb[j], rows, not last)
            w_out = d_w_out[j]
        n_rows = rows.n_lat if last else rows.n_all
        x_all, u = _mm_ln(o, w_out.astype(BF16), x_all, mods, i, 2, ln_g[i, 0].astype(F32), ln_b[i, 0].astype(F32),
                          rows, next_mod=(i, 3, 4), n_rows=n_rows)
        hidden = _proj(u, mlp_w1[i].astype(BF16), 0, D_FF, rows, mode="relu2", n_rows=n_rows)
        x_all, u = _mm_ln(hidden, mlp_w2[i].astype(BF16), x_all, mods, i, 5, ln_g[i, 1].astype(F32),
                          ln_b[i, 1].astype(F32), rows, next_mod=None if last else (i + 1, 0, 1), n_rows=n_rows)
    return x_all.reshape(batch, seq, d).astype(x.dtype)
```

```python
import functools
import math

import jax
import jax.numpy as jnp
from jax import lax
from jax.experimental import pallas as pl
from jax.experimental.pallas import tpu as pltpu

F32 = jnp.float32
BF16 = jnp.bfloat16

D_MODEL = 2048
DEPTH = 4
N_MIXERS = 4
GRID_W = 64
D_FF = 4 * D_MODEL
ROPE_BASE = 10000.0
EPS = 1e-6
DN_ALPHA = (2.0 * DEPTH) ** 0.25
N_HEADS = D_MODEL // 128
HEAD = 128
A_DK = 64
B_KV_HEADS = 4
B_GROUP = N_HEADS // B_KV_HEADS
C_DI = 2 * D_MODEL
C_HD = 64
C_GROUPS = 8
C_HPG = (C_DI // C_HD) // C_GROUPS
C_STATE = 128
C_CONV = 5
C_CHUNK = 128
C_CONV_DIM = C_DI + 2 * C_GROUPS * C_STATE
NA_ROWS = 8
NA_COLS = 16
NA_QROWS = 8
NA_WROWS = NA_QROWS + NA_ROWS
MASKED = -1e30
LOG2E = math.log2(math.e)
SSD_GROUPS_PER_STEP = 2
ATTN_SUB_ROWS = 256
ATTN_KEY_CHUNK = 512

V7X_VMEM_BYTES = 64 * 1024 * 1024
VMEM_LIMIT = V7X_VMEM_BYTES - 8 * 1024 * 1024
LANES = 128
SUBLANES = 8


def _params(*sem):
    return pltpu.CompilerParams(dimension_semantics=sem, vmem_limit_bytes=VMEM_LIMIT)


def _dot(a, b):
    return jnp.dot(a, b, preferred_element_type=F32)


def _dot_nt(a, b, precision=None):
    return lax.dot_general(a, b, (((1,), (1,)), ((), ())), preferred_element_type=F32, precision=precision)


def _silu(x):
    return x / (1.0 + jnp.exp(-x))


def _ada_kernel(c_ref, w_ref, b_ref, o_ref):
    sc = _silu(c_ref[...]).astype(BF16)
    o_ref[...] = _dot(sc, w_ref[...].astype(BF16)) + b_ref[...]


def _ada_mods(cvec, ada_w, ada_b):
    depth, d, n = ada_w.shape
    bn = 1024
    out = pl.pallas_call(
        _ada_kernel,
        grid=(depth, n // bn),
        in_specs=[
            pl.BlockSpec((SUBLANES, d), lambda l, j: (0, 0)),
            pl.BlockSpec((None, d, bn), lambda l, j: (l, 0, j)),
            pl.BlockSpec((None, 1, bn), lambda l, j: (l, 0, j)),
        ],
        out_specs=pl.BlockSpec((None, SUBLANES, bn), lambda l, j: (l, 0, j)),
        out_shape=jax.ShapeDtypeStruct((depth, SUBLANES, n), F32),
        compiler_params=_params("parallel", "parallel"),
        name="ada_mods",
    )(cvec, ada_w, ada_b.reshape(depth, 1, n))
    return out.reshape(depth, SUBLANES, 6, 1, d)


class _Rows:
    def __init__(self, batch, seq, ctx, bm):
        self.batch, self.seq, self.ctx, self.bm = batch, seq, ctx, bm
        self.n_lat = batch * seq
        self.n_all = batch * (seq + ctx)
        assert seq % bm == 0 and (batch * ctx) % bm == 0

    def mod_row(self, i):
        return jnp.minimum((i * self.bm) // self.seq, self.batch)

    def mod_spec(self, mods, layer, k, grid_pos):
        d = mods.shape[-1]
        return pl.BlockSpec((None, None, None, 1, d),
                            lambda *g: (layer, self.mod_row(g[grid_pos]), k, 0, 0))


def _modulate_kernel(x_ref, sh_ref, sc_ref, u_ref):
    u_ref[...] = (x_ref[...] * (1.0 + sc_ref[...]) + sh_ref[...]).astype(u_ref.dtype)


def _modulate(x_all, mods, layer, rows):
    m, d = x_all.shape
    bm = rows.bm
    return pl.pallas_call(
        _modulate_kernel,
        grid=(m // bm,),
        in_specs=[pl.BlockSpec((bm, d), lambda i: (i, 0)),
                  rows.mod_spec(mods, layer, 0, 0), rows.mod_spec(mods, layer, 1, 0)],
        out_specs=pl.BlockSpec((bm, d), lambda i: (i, 0)),
        out_shape=jax.ShapeDtypeStruct((m, d), BF16),
        compiler_params=_params("parallel"),
        name="modulate",
    )(x_all, mods, mods)


def _proj_kernel(*refs, mode, scale, norm):
    if mode == "rope":
        if norm:
            u_ref, w_ref, cos_ref, sin_ref, g_ref, o_ref = refs
        else:
            u_ref, w_ref, cos_ref, sin_ref, o_ref = refs
    else:
        u_ref, w_ref, o_ref = refs
    acc = _dot(u_ref[...], w_ref[...])
    if mode == "relu2":
        r = jnp.maximum(acc, 0.0)
        o_ref[...] = (r * r).astype(o_ref.dtype)
    elif mode == "plain":
        if scale != 1.0:
            acc = acc * scale
        o_ref[...] = acc.astype(o_ref.dtype)
    else:
        cos = cos_ref[...]
        sin = sin_ref[...]
        for c in range(acc.shape[1] // HEAD):
            x = acc[:, c * HEAD:(c + 1) * HEAD]
            if norm:
                x = x * lax.rsqrt(jnp.mean(x * x, axis=1, keepdims=True) + EPS) * g_ref[...]
            x = x * cos + pltpu.roll(x, HEAD // 2, 1) * sin
            if scale != 1.0:
                x = x * scale
            o_ref[:, c * HEAD:(c + 1) * HEAD] = x.astype(o_ref.dtype)


def _proj(u, w, col0, ncols, rows, *, mode="plain", scale=1.0, rope=None, gain=None,
          out_dtype=BF16, bn=1024, n_rows=None):
    m = u.shape[0] if n_rows is None else n_rows
    k = u.shape[1]
    bm = rows.bm
    bn = min(bn, ncols)
    assert ncols % bn == 0 and col0 % bn == 0 and m % bm == 0
    cb0 = col0 // bn
    in_specs = [pl.BlockSpec((bm, k), lambda j, i: (i, 0)),
                pl.BlockSpec((k, bn), lambda j, i: (0, cb0 + j))]
    args = [u, w]
    if mode == "rope":
        cos, sin = rope
        n_seq = rows.seq // bm
        n_lat = rows.n_lat // bm

        def tab(j, i):
            return (jnp.where(i < n_lat, i % n_seq, n_seq), 0)

        in_specs += [pl.BlockSpec((bm, HEAD), tab), pl.BlockSpec((bm, HEAD), tab)]
        args += [cos, sin]
        if gain is not None:
            in_specs.append(pl.BlockSpec((1, HEAD), lambda j, i: (0, 0)))
            args.append(gain)
    return pl.pallas_call(
        functools.partial(_proj_kernel, mode=mode, scale=scale, norm=gain is not None),
        grid=(ncols // bn, m // bm),
        in_specs=in_specs,
        out_specs=pl.BlockSpec((bm, bn), lambda j, i: (i, j)),
        out_shape=jax.ShapeDtypeStruct((m, ncols), out_dtype),
        compiler_params=_params("parallel", "parallel"),
        name="proj_" + mode,
    )(*args)


def _mm_ln_kernel(*refs, nk, emit_u, n_lat_blocks):
    refs = list(refs)
    if n_lat_blocks is None:
        a_ref = refs.pop(0)
        load_a = lambda: a_ref[...]
    else:
        a_lat_ref, a_ctx_ref = refs.pop(0), refs.pop(0)
        load_a = lambda: jnp.where(pl.program_id(0) < n_lat_blocks, a_lat_ref[...], a_ctx_ref[...])

    if emit_u:
        w_ref, x_ref, gate_ref, g_ref, b_ref, sh_ref, sc_ref, xo_ref, uo_ref = refs[:9]
        rest = refs[9:]
    else:
        w_ref, x_ref, gate_ref, g_ref, b_ref, xo_ref = refs[:6]
        rest = refs[6:]

    def finish(y):
        v = DN_ALPHA * x_ref[...] + gate_ref[...] * y
        mu = jnp.mean(v, axis=1, keepdims=True)
        vc = v - mu
        var = jnp.mean(vc * vc, axis=1, keepdims=True)
        o = vc * lax.rsqrt(var + EPS) * g_ref[...] + b_ref[...]
        xo_ref[...] = o
        if emit_u:
            uo_ref[...] = (o * (1.0 + sc_ref[...]) + sh_ref[...]).astype(uo_ref.dtype)

    if nk == 1:
        finish(_dot(load_a(), w_ref[...]))
    else:
        acc_ref, = rest
        kk = pl.program_id(1)

        @pl.when(kk == 0)
        def _():
            acc_ref[...] = _dot(load_a(), w_ref[...])

        @pl.when(jnp.logical_and(kk > 0, kk < nk - 1))
        def _():
            acc_ref[...] = _dot(load_a(), w_ref[...]) + acc_ref[...]

        @pl.when(kk == nk - 1)
        def _():
            finish(_dot(load_a(), w_ref[...]) + acc_ref[...])


def _mm_ln(a, w, x_res, mods, layer, gate_k, ln_g, ln_b, rows, *, next_mod=None, n_rows=None):
    split = isinstance(a, tuple)
    a_parts = a if split else (a,)
    m = sum(p.shape[0] for p in a_parts) if n_rows is None else n_rows
    k = a_parts[0].shape[1]
    d = w.shape[1]
    if k <= 2048:
        bm, bk = 256, k
    else:
        bm, bk = 512, 2048
    bm = min(bm, rows.bm)
    nk = k // bk
    sub = _Rows(rows.batch, rows.seq, rows.ctx, bm)
    emit_u = next_mod is not None
    vec = pl.BlockSpec((1, d), lambda i, kk: (0, 0))
    if split:
        n_lat_blocks = a_parts[0].shape[0] // bm
        assert a_parts[0].shape[0] % bm == 0 and a_parts[1].shape[0] % bm == 0
        in_specs = [pl.BlockSpec((bm, bk), lambda i, kk: (jnp.minimum(i, n_lat_blocks - 1), kk)),
                    pl.BlockSpec((bm, bk), lambda i, kk: (jnp.maximum(i - n_lat_blocks, 0), kk))]
    else:
        n_lat_blocks = None
        in_specs = [pl.BlockSpec((bm, bk), lambda i, kk: (i, kk))]
    in_specs += [pl.BlockSpec((bk, d), lambda i, kk: (kk, 0)),
                 pl.BlockSpec((bm, d), lambda i, kk: (i, 0)),
                 sub.mod_spec(mods, layer, gate_k, 0), vec, vec]
    args = [*a_parts, w, x_res, mods, ln_g.reshape(1, d), ln_b.reshape(1, d)]
    out_specs = [pl.BlockSpec((bm, d), lambda i, kk: (i, 0))]
    out_shape = [jax.ShapeDtypeStruct((m, d), F32)]
    if emit_u:
        nl, ksh, ksc = next_mod
        in_specs += [sub.mod_spec(mods, nl, ksh, 0), sub.mod_spec(mods, nl, ksc, 0)]
        args += [mods, mods]
        out_specs.append(pl.BlockSpec((bm, d), lambda i, kk: (i, 0)))
        out_shape.append(jax.ShapeDtypeStruct((m, d), BF16))
    res = pl.pallas_call(
        functools.partial(_mm_ln_kernel, nk=nk, emit_u=emit_u, n_lat_blocks=n_lat_blocks),
        grid=(m // bm, nk),
        in_specs=in_specs,
        out_specs=out_specs,
        out_shape=out_shape,
        scratch_shapes=[pltpu.VMEM((bm, d), F32)] if nk > 1 else [],
        compiler_params=_params("parallel", "arbitrary"),
        name="mm_ln",
    )(*args)
    return (res[0], res[1]) if emit_u else (res[0], None)


def _softmax_pipeline(n_items, n_rows, segments, score, value, s_ref):
    def lane_groups(t):
        return [t[:, g * LANES:(g + 1) * LANES] for g in range(t.shape[1] // LANES)]

    def pass1(i, seg, mxv):
        s = score(i, seg)
        col = seg[-1]
        s_ref[i % 2, :, col:col + s.shape[1]] = s
        for part in lane_groups(s):
            mxv = part if mxv is None else jnp.maximum(mxv, part)
        return mxv

    def pass2(i, seg, mx, acc):
        v = value(seg)
        col = seg[-1]
        p = jnp.exp2((s_ref[i % 2, :, col:col + v.shape[0]] - mx).astype(BF16))
        return acc + _dot(p, jnp.concatenate([v, jnp.ones_like(v)], axis=1))

    outs = []
    mxv = None
    for seg in segments:
        mxv = pass1(0, seg, mxv)
    for i in range(n_items):
        mx = jnp.max(mxv, axis=1, keepdims=True)
        acc, mxv = jnp.zeros((n_rows, 2 * LANES), F32), None
        for seg in segments:
            if i + 1 < n_items:
                mxv = pass1(i + 1, seg, mxv)
            acc = pass2(i, seg, mx, acc)
        outs.append(acc[:, :LANES] / acc[:, LANES:])
    return outs


def _attn_kernel(*refs, n_maps, with_lat, lam_init, key_chunk, sub_rows):
    refs = list(refs)
    if n_maps == 2:
        lam_ref = refs.pop(0)
    q_ref, kc_ref, vc_ref = refs[:3]
    refs = refs[3:]
    if with_lat:
        kl_ref, vl_ref = refs[:2]
        refs = refs[2:]
    if n_maps == 2:
        g_ref = refs.pop(0)
    o_ref, s_ref = refs

    segments = [(kc_ref, vc_ref, 0, kc_ref.shape[0], 0)]
    if with_lat:
        n_ctx = kc_ref.shape[0]
        segments += [(kl_ref, vl_ref, c * key_chunk, key_chunk, n_ctx + c * key_chunk)
                     for c in range(kl_ref.shape[0] // key_chunk)]
    n_sub = q_ref.shape[0] // sub_rows
    items = []
    for r in range(n_sub):
        q = q_ref[r * sub_rows:(r + 1) * sub_rows, :]
        for mi in range(n_maps):
            if n_maps == 2:
                lane = lax.broadcasted_iota(jnp.int32, q.shape, 1)
                items.append(jnp.where((lane // 32) % 2 == mi, q, jnp.zeros_like(q)))
            else:
                items.append(q)

    def score(i, seg):
        k_ref, _, off, n, _ = seg
        return _dot_nt(items[i], k_ref[off:off + n, :])

    def value(seg):
        _, v_ref, off, n, _ = seg
        return v_ref[off:off + n, :]

    outs = _softmax_pipeline(len(items), sub_rows, segments, score, value, s_ref)

    if n_maps == 2:
        lv = lam_ref[...]
        lam = (jnp.exp(jnp.sum(lv[0:1] * lv[1:2], axis=1, keepdims=True))
               - jnp.exp(jnp.sum(lv[2:3] * lv[3:4], axis=1, keepdims=True)) + lam_init)
    for r in range(n_sub):
        if n_maps == 2:
            o = outs[2 * r] - lam * outs[2 * r + 1]
            o = o * lax.rsqrt(jnp.mean(o * o, axis=1, keepdims=True) + EPS) * g_ref[...] * (1.0 - lam_init)
        else:
            o = outs[r]
        o_ref[r * sub_rows:(r + 1) * sub_rows, :] = o.astype(o_ref.dtype)


def _attention(q, k, v, rows, *, kv_group, n_maps=1, lam_vec=None, sub_g=None, lam_init=0.0,
               ctx_queries=False, key_chunk=ATTN_KEY_CHUNK, sub_rows=ATTN_SUB_ROWS, items_per_step=4):
    batch, seq, ctx = rows.batch, rows.seq, rows.ctx
    assert seq % min(key_chunk, seq) == 0
    bq = sub_rows * max(items_per_step // n_maps, 1)
    while seq % bq:
        bq //= 2
    n_heads = q.shape[1] // HEAD
    lat_blocks = rows.n_lat // ctx
    if ctx_queries:
        bq = ctx
        nq = 1
        q_map = lambda b, h, j: (lat_blocks + b, h)
        m_out = batch * ctx
    else:
        nq = seq // bq
        q_map = lambda b, h, j: (b * nq + j, h)
        m_out = rows.n_lat
    kc_map = lambda b, h, j: (lat_blocks + b, h // kv_group)
    in_specs, args = [], []
    if n_maps == 2:
        in_specs.append(pl.BlockSpec(lam_vec.shape, lambda b, h, j: (0, 0)))
        args.append(lam_vec)
    in_specs += [pl.BlockSpec((bq, HEAD), q_map),
                 pl.BlockSpec((ctx, HEAD), kc_map), pl.BlockSpec((ctx, HEAD), kc_map)]
    args += [q, k, v]
    if not ctx_queries:
        kl_map = lambda b, h, j: (b, h // kv_group)
        in_specs += [pl.BlockSpec((seq, HEAD), kl_map), pl.BlockSpec((seq, HEAD), kl_map)]
        args += [k, v]
    if n_maps == 2:
        in_specs.append(pl.BlockSpec((1, HEAD), lambda b, h, j: (0, 0)))
        args.append(sub_g)
    o_map = (lambda b, h, j: (b, h)) if ctx_queries else (lambda b, h, j: (b * nq + j, h))
    return pl.pallas_call(
        functools.partial(_attn_kernel, n_maps=n_maps, with_lat=not ctx_queries, lam_init=lam_init,
                          key_chunk=min(key_chunk, seq), sub_rows=min(sub_rows, bq)),
        grid=(batch, n_heads, nq),
        in_specs=in_specs,
        out_specs=pl.BlockSpec((bq, HEAD), o_map),
        out_shape=jax.ShapeDtypeStruct((m_out, n_heads * HEAD), BF16),
        scratch_shapes=[pltpu.VMEM((2, min(sub_rows, bq), ctx if ctx_queries else ctx + seq), F32)],
        compiler_params=_params("parallel", "parallel", "arbitrary"),
        name="attn_ctx" if ctx_queries else "attn_lat",
    )(*args)


def _na_window_start(kind, a):
    if kind == 0:
        return max(a - NA_ROWS // 2, 0), 0
    if kind == 1:
        return a, NA_ROWS // 2
    return min(a + NA_ROWS // 2, NA_WROWS - NA_ROWS), NA_WROWS - NA_QROWS


def _na_build_bias(tiles_ref, bias_ref, kind):
    w = GRID_W
    lane = lax.broadcasted_iota(jnp.int32, (w, 2 * w), 1)
    masked = jnp.full((w, 2 * w), MASKED, F32)
    for a in range(NA_QROWS):
        lo, off = _na_window_start(kind, a)
        for c in range(0, NA_WROWS, 2):
            ok0 = lo <= c < lo + NA_ROWS
            ok1 = lo <= c + 1 < lo + NA_ROWS
            if ok0 or ok1:
                i0 = c - a - off + NA_ROWS - 1
                assert 0 <= i0 <= 2 * NA_ROWS - 2
                tile = tiles_ref[i0]
                if not ok1:
                    tile = jnp.where(lane < w, tile, MASKED)
                elif not ok0:
                    tile = jnp.where(lane >= w, tile, MASKED)
            else:
                tile = masked
            bias_ref[kind, a * w:(a + 1) * w, c * w:(c + 2) * w] = tile


def _na_kernel(q_ref, k_ref, v_ref, kc_ref, vc_ref, tiles_ref, o_ref, bias_ref, s_ref, *, n_rows,
               key_chunk=ATTN_KEY_CHUNK, sub_rows=ATTN_SUB_ROWS):
    b = pl.program_id(1)
    j = pl.program_id(2)
    nb = n_rows // NA_QROWS
    wlen = NA_WROWS * GRID_W

    for kind, jj in ((0, 0), (1, 1), (2, nb - 1)):
        @pl.when(jnp.logical_and(b == 0, j == jj))
        def _():
            _na_build_bias(tiles_ref, bias_ref, kind)

    kind = jnp.where(j == 0, 0, jnp.where(j == nb - 1, 2, 1))
    start = pl.multiple_of(jnp.clip(NA_QROWS * j - NA_ROWS // 2, 0, n_rows - NA_WROWS) * GRID_W, 4 * GRID_W)
    n_ctx = kc_ref.shape[0]
    n_items = q_ref.shape[0] // sub_rows
    segments = [(False, 0, n_ctx, 0)]
    segments += [(True, c * key_chunk, key_chunk, n_ctx + c * key_chunk) for c in range(wlen // key_chunk)]

    def score(i, seg):
        window, off, n, _ = seg
        q = q_ref[i * sub_rows:(i + 1) * sub_rows, :]
        if not window:
            return _dot_nt(q, kc_ref[...])
        k = k_ref[pl.ds(start + off, n), :]
        return _dot_nt(q, k) + bias_ref[kind, i * sub_rows:(i + 1) * sub_rows, off:off + n]

    def value(seg):
        window, off, n, _ = seg
        return v_ref[pl.ds(start + off, n), :] if window else vc_ref[...]

    outs = _softmax_pipeline(n_items, sub_rows, segments, score, value, s_ref)
    for i, o in enumerate(outs):
        o_ref[i * sub_rows:(i + 1) * sub_rows, :] = o.astype(o_ref.dtype)


def _na_tiles_kernel(rpb_ref, o_ref):
    n = o_ref.shape[1]
    shift = GRID_W.bit_length() - 1
    jj = lax.broadcasted_iota(jnp.int32, (LANES, n), 0)
    pos = lax.broadcasted_iota(jnp.int32, (LANES, n), 1)
    rel = (pos & (GRID_W - 1)) - (pos >> shift) + (NA_COLS - 1)
    onehot = (rel == jj).astype(F32)
    vals = jnp.dot(rpb_ref[...], onehot, preferred_element_type=F32, precision=lax.Precision.HIGHEST)
    pos1 = lax.broadcasted_iota(jnp.int32, (1, n), 1)
    qc = pos1 >> shift
    kc = pos1 & (GRID_W - 1)
    d = kc - jnp.clip(qc - NA_COLS // 2, 0, GRID_W - NA_COLS)
    inside = jnp.logical_and(d >= 0, d < NA_COLS)
    o_ref[...] = jnp.where(inside, vals * LOG2E, MASKED)


def _na_tiles(rpb):
    n_heads, n_ri, n_ci = rpb.shape
    assert n_ri == 2 * NA_ROWS - 1 and n_ci == 2 * NA_COLS - 1 and n_ci <= LANES and n_ri < 2 * SUBLANES
    padded = jnp.pad(rpb.astype(F32), ((0, 0), (0, 2 * SUBLANES - n_ri), (0, LANES - n_ci)))
    n = GRID_W * GRID_W
    flat = pl.pallas_call(
        _na_tiles_kernel,
        grid=(n_heads,),
        in_specs=[pl.BlockSpec((None, 2 * SUBLANES, LANES), lambda h: (h, 0, 0))],
        out_specs=pl.BlockSpec((None, 2 * SUBLANES, n), lambda h: (h, 0, 0)),
        out_shape=jax.ShapeDtypeStruct((n_heads, 2 * SUBLANES, n), F32),
        compiler_params=_params("parallel"),
        name="na_tiles",
    )(padded)
    t = flat.reshape(n_heads, 2 * SUBLANES, GRID_W, GRID_W)
    return jnp.concatenate([t[:, :n_ri], t[:, 1:n_ri + 1]], axis=-1)


def _neighbourhood(q, k, v, rpb, rows):
    batch, seq, ctx = rows.batch, rows.seq, rows.ctx
    n_rows = seq // GRID_W
    assert n_rows % NA_QROWS == 0 and n_rows >= NA_WROWS
    nb = n_rows // NA_QROWS
    bq = NA_QROWS * GRID_W
    wlen = NA_WROWS * GRID_W
    lat_blocks = rows.n_lat // ctx
    tiles = _na_tiles(rpb)
    return pl.pallas_call(
        functools.partial(_na_kernel, n_rows=n_rows),
        grid=(N_HEADS, batch, nb),
        in_specs=[pl.BlockSpec((bq, HEAD), lambda h, b, j: (b * nb + j, h)),
                  pl.BlockSpec((seq, HEAD), lambda h, b, j: (b, h)),
                  pl.BlockSpec((seq, HEAD), lambda h, b, j: (b, h)),
                  pl.BlockSpec((ctx, HEAD), lambda h, b, j: (lat_blocks + b, h)),
                  pl.BlockSpec((ctx, HEAD), lambda h, b, j: (lat_blocks + b, h)),
                  pl.BlockSpec((None,) + tiles.shape[1:], lambda h, b, j: (h, 0, 0, 0))],
        out_specs=pl.BlockSpec((bq, HEAD), lambda h, b, j: (b * nb + j, h)),
        out_shape=jax.ShapeDtypeStruct((rows.n_lat, N_HEADS * HEAD), BF16),
        scratch_shapes=[pltpu.VMEM((3, bq, wlen), F32), pltpu.VMEM((2, ATTN_SUB_ROWS, ctx + wlen), F32)],
        compiler_params=_params("arbitrary", "arbitrary", "arbitrary"),
        name="na_attn",
    )(q, k, v, k, v, tiles)


def _conv_kernel(prev_ref, cur_ref, next_ref, w_ref, b_ref, o_ref, ext_ref, *, blocks_per_seq, n_lat_blocks):
    i = pl.program_id(1)
    bm = cur_ref.shape[0]
    pos = i % blocks_per_seq
    is_lat = i < n_lat_blocks
    first = jnp.logical_or(jnp.logical_not(is_lat), pos == 0)
    last = jnp.logical_or(jnp.logical_not(is_lat), pos == blocks_per_seq - 1)
    halo_p = prev_ref[...].astype(F32)
    halo_n = next_ref[...].astype(F32)
    ext_ref[0:SUBLANES, :] = jnp.where(first, jnp.zeros_like(halo_p), halo_p)
    ext_ref[SUBLANES:SUBLANES + bm, :] = cur_ref[...].astype(F32)
    ext_ref[SUBLANES + bm:2 * SUBLANES + bm, :] = jnp.where(last, jnp.zeros_like(halo_n), halo_n)
    w = w_ref[...]
    acc = jnp.zeros(o_ref.shape, F32) + b_ref[...]
    for t in range(C_CONV):
        off = SUBLANES - C_CONV // 2 + t
        acc = acc + ext_ref[off:off + bm, :] * w[t:t + 1, :]
    o_ref[...] = _silu(acc).astype(o_ref.dtype)


def _conv_silu(xbc, conv_w, conv_b, rows, bc=2048):
    m, n = xbc.shape
    bm = rows.ctx
    assert rows.seq % bm == 0 and bm % SUBLANES == 0
    hb = bm // SUBLANES
    n8 = m // SUBLANES
    return pl.pallas_call(
        functools.partial(_conv_kernel, blocks_per_seq=rows.seq // bm, n_lat_blocks=rows.n_lat // bm),
        grid=(n // bc, m // bm),
        in_specs=[pl.BlockSpec((SUBLANES, bc), lambda j, i: (jnp.maximum(i * hb - 1, 0), j)),
                  pl.BlockSpec((bm, bc), lambda j, i: (i, j)),
                  pl.BlockSpec((SUBLANES, bc), lambda j, i: (jnp.minimum((i + 1) * hb, n8 - 1), j)),
                  pl.BlockSpec((C_CONV, bc), lambda j, i: (0, j)),
                  pl.BlockSpec((1, bc), lambda j, i: (0, j))],
        out_specs=pl.BlockSpec((bm, bc), lambda j, i: (i, j)),
        out_shape=jax.ShapeDtypeStruct((m, n), BF16),
        scratch_shapes=[pltpu.VMEM((bm + 2 * SUBLANES, bc), F32)],
        compiler_params=_params("parallel", "arbitrary"),
        name="conv_silu",
    )(xbc, xbc, xbc, conv_w, conv_b.reshape(1, n))


def _ssd_chunk(x_ref, b_ref, c_ref, dt_ref, alog_ref, dtb_ref, dsk_ref, y_ref, h_ref, forward):
    q = C_CHUNK
    half = q // 2
    assert q == 2 * C_HD and C_HPG % 2 == 0
    row = lax.broadcasted_iota(jnp.int32, (q, q), 0)
    col = lax.broadcasted_iota(jnp.int32, (q, q), 1)
    before = (col <= row) if forward else (col >= row)

    def thrice(mask):
        m = jnp.where(mask, 1.0, 0.0).astype(BF16)
        return jnp.concatenate([m, m, m], axis=1)

    def pieces(v):
        hi = v.astype(BF16).astype(F32)
        r = v - hi
        mid = r.astype(BF16).astype(F32)
        return hi, mid, r - mid

    def per_channel(v):
        rep = lambda t: jnp.concatenate([jnp.broadcast_to(t[h:h + 1, :], (C_HD, q)) for h in range(C_HPG)], axis=0)
        return jnp.concatenate([rep(t).astype(BF16) for t in pieces(v)], axis=1)

    dt_row = dt_ref[...] + dtb_ref[...]
    dt_row = jnp.maximum(dt_row, 0.0) + jnp.log1p(jnp.exp(-jnp.abs(dt_row)))
    dta_row = dt_row * (-jnp.exp(alog_ref[...]))
    yield
    before3 = thrice(before)
    cs_e = _dot_nt(before3, per_channel(dta_row))
    dt_e = _dot_nt(thrice(row == col), per_channel(dt_row))
    pad = jnp.zeros_like(dta_row)
    dta16 = jnp.concatenate([jnp.concatenate([t, pad], axis=0).astype(BF16) for t in pieces(dta_row)], axis=1)
    cs_row = _dot_nt(dta16, before3)[:C_HPG]
    end_e = cs_e[q - 1:q, :] if forward else cs_e[0:1, :]
    yield

    x = x_ref[...].astype(F32)
    xdt = x * dt_e
    bc = b_ref[...]
    cc = c_ref[...]
    cb = _dot_nt(cc, bc)
    h_old = h_ref[...]
    y = _dot(cc, h_old.astype(BF16)) * jnp.exp(cs_e) + dsk_ref[...] * x
    yield

    lane = lax.broadcasted_iota(jnp.int32, (1, q), 1)
    same_half = (row < half) == (col < half)
    parts = []
    for h0 in range(0, C_HPG, 2):
        cp = cs_e[:, h0 * C_HD:(h0 + 2) * C_HD]
        r1 = jnp.where(lane < half, cs_row[h0:h0 + 1, :], cs_row[h0 + 1:h0 + 2, :])
        r2 = jnp.where(lane < half, cs_row[h0 + 1:h0 + 2, :], cs_row[h0:h0 + 1, :])
        u1 = cb * jnp.exp(jnp.where(before, cp - r1, MASKED))
        u2 = cb * jnp.exp(jnp.where(before, pltpu.roll(cp, half, 1) - r2, MASKED))
        xp = xdt[:, h0 * C_HD:(h0 + 2) * C_HD]
        rhs = jnp.concatenate([jnp.where(same_half, xp, 0.0), jnp.where(same_half, 0.0, xp)], axis=0)
        parts.append(_dot(jnp.concatenate([u1, u2], axis=1).astype(BF16), rhs.astype(BF16)))
        yield
    y_ref[...] = (y + jnp.concatenate(parts, axis=1)).astype(y_ref.dtype)

    bct = bc.astype(F32).T.astype(BF16)
    h_ref[...] = h_old * jnp.exp(end_e) + _dot(bct, (xdt * jnp.exp(end_e - cs_e)).astype(BF16))


def _ssd_kernel(*refs):
    y_refs, h_ref = refs[14:16], refs[16]
    hp = C_HPG * C_HD

    @pl.when(pl.program_id(2) == 0)
    def _():
        h_ref[...] = jnp.zeros_like(h_ref)

    chains = []
    for d in range(2):
        x_ref, b_ref, c_ref, dt_ref, alog_ref, dtb_ref, dsk_ref = refs[7 * d:7 * d + 7]
        for g in range(SSD_GROUPS_PER_STEP):
            wide = (slice(None), pl.ds(g * hp, hp))
            state = (slice(None), pl.ds(g * C_STATE, C_STATE))
            head = (pl.ds(g * C_HPG, C_HPG), slice(None))
            chains.append(_ssd_chunk(x_ref.at[wide], b_ref.at[state], c_ref.at[state], dt_ref.at[head],
                                     alog_ref.at[head], dtb_ref.at[head], dsk_ref.at[wide], y_refs[d].at[wide],
                                     h_ref.at[d, g], d == 0))
    while chains:
        chains = [c for c in chains if next(c, StopIteration) is not StopIteration]


def _ssd(xbc, dt_t, a_log, dt_bias, d_skip, rows):
    batch, seq, ctx = rows.batch, rows.seq, rows.ctx
    m = xbc.shape[0]
    q = C_CHUNK
    nc_ctx = ctx // q
    nc_lat = seq // q
    nc = nc_ctx + nc_lat
    lat_chunks = rows.n_lat // q
    hp = C_HPG * C_HD

    def chunk(d):
        def index(b, s):
            c_ctx = s if d == 0 else nc_ctx - 1 - s
            c_lat = s - nc_ctx if d == 0 else nc - 1 - s
            return jnp.where(s < nc_ctx, lat_chunks + b * nc_ctx + c_ctx, b * nc_lat + c_lat)
        return index

    gps = SSD_GROUPS_PER_STEP
    ng = C_GROUPS // gps
    n_xb = C_DI // (gps * C_STATE)
    n_bb = ng
    heads = 2 * C_GROUPS * C_HPG
    col = lambda t: t.astype(F32).reshape(heads, 1)
    dsk_e = jnp.repeat(d_skip.astype(F32).reshape(2, 1, C_GROUPS * C_HPG), C_HD, axis=2)
    in_specs, args, out_specs = [], [], []
    for d in range(2):
        ch = chunk(d)
        hspec = pl.BlockSpec((gps * C_HPG, 1), lambda b, g, s, d=d: (d * ng + g, 0))
        in_specs += [pl.BlockSpec((q, gps * hp), lambda b, g, s, ch=ch: (ch(b, s), g)),
                     pl.BlockSpec((q, gps * C_STATE), lambda b, g, s, ch=ch: (ch(b, s), n_xb + g)),
                     pl.BlockSpec((q, gps * C_STATE), lambda b, g, s, ch=ch: (ch(b, s), n_xb + n_bb + g)),
                     pl.BlockSpec((gps * C_HPG, q), lambda b, g, s, ch=ch, d=d: (d * ng + g, ch(b, s))),
                     hspec, hspec,
                     pl.BlockSpec((None, 1, gps * hp), lambda b, g, s, d=d: (d, 0, g))]
        args += [xbc, xbc, xbc, dt_t, col(a_log), col(dt_bias), dsk_e]
        out_specs.append(pl.BlockSpec((q, gps * hp), lambda b, g, s, ch=ch: (ch(b, s), g)))
    return pl.pallas_call(
        _ssd_kernel,
        grid=(batch, ng, nc),
        in_specs=in_specs,
        out_specs=out_specs,
        out_shape=[jax.ShapeDtypeStruct((m, C_DI), BF16)] * 2,
        scratch_shapes=[pltpu.VMEM((2, gps, C_STATE, hp), F32)],
        compiler_params=_params("parallel", "parallel", "arbitrary"),
        name="ssd_scan",
    )(*args)


def _gate_norm_kernel(yf_ref, yb_ref, z_ref, g_ref, o_ref):
    v = (yf_ref[...].astype(F32) + yb_ref[...].astype(F32)) * _silu(z_ref[...].astype(F32))
    o_ref[...] = (v * lax.rsqrt(jnp.mean(v * v, axis=1, keepdims=True) + EPS) * g_ref[...]).astype(o_ref.dtype)


def _gate_norm(y_fwd, y_bwd, z, norm_g, bm=256):
    m, n = z.shape
    blk = pl.BlockSpec((bm, n), lambda i: (i, 0))
    return pl.pallas_call(
        _gate_norm_kernel,
        grid=(m // bm,),
        in_specs=[blk, blk, blk, pl.BlockSpec((1, n), lambda i: (0, 0))],
        out_specs=blk,
        out_shape=jax.ShapeDtypeStruct((m, n), BF16),
        compiler_params=_params("parallel"),
        name="gate_norm",
    )(y_fwd, y_bwd, z, norm_g.astype(F32).reshape(1, n))


def _rope_tables(seq, dim, n_copies, bm):
    t = jnp.arange(seq, dtype=jnp.int32)
    row = (t // GRID_W).astype(F32)
    colp = (t % GRID_W).astype(F32)
    n_pairs = dim // 4
    inv = ROPE_BASE ** (-jnp.arange(n_pairs, dtype=F32) / n_pairs)
    ang = jnp.concatenate([row[:, None] * inv, colp[:, None] * inv], axis=-1)
    cos = jnp.tile(jnp.cos(ang), (1, 2 * n_copies))
    sin = jnp.tile(jnp.sin(ang), (1, n_copies))
    sin = jnp.concatenate([-sin, sin], axis=-1)
    cos = jnp.concatenate([cos, jnp.ones((bm, HEAD), F32)], axis=0)
    sin = jnp.concatenate([sin, jnp.zeros((bm, HEAD), F32)], axis=0)
    return cos, sin


def _deinterleave_perm(n_maps):
    dim = HEAD // n_maps
    perm = []
    for e in range(2):
        for mi in range(n_maps):
            for p in range(dim // 2):
                perm.append(mi * dim + 2 * p + e)
    return perm


def _head_perm(n_heads, n_maps):
    base = _deinterleave_perm(n_maps)
    return jnp.asarray([h * HEAD + c for h in range(n_heads) for c in base], dtype=jnp.int32)


def _mixer_a(u, w_in, lam_vec, sub_g, rows, lam_init, need_ctx):
    d = D_MODEL
    perm = _head_perm(N_HEADS, 2)
    w_q = w_in[:, :d][:, perm].astype(BF16)
    w_k = w_in[:, d:2 * d][:, perm].astype(BF16)
    rope = _rope_tables(rows.seq, A_DK, 2, rows.bm)
    qh = _proj(u, w_q, 0, d, rows, mode="rope", rope=rope, scale=A_DK ** -0.5 * LOG2E)
    kh = _proj(u, w_k, 0, d, rows, mode="rope", rope=rope)
    vh = _proj(u, w_in.astype(BF16), 2 * d, d, rows)
    kw = dict(kv_group=1, n_maps=2, lam_vec=lam_vec.astype(F32), sub_g=sub_g.astype(F32).reshape(1, HEAD),
              lam_init=lam_init)
    o_l = _attention(qh, kh, vh, rows, **kw)
    if not need_ctx:
        return o_l
    return o_l, _attention(qh, kh, vh, rows, ctx_queries=True, **kw)


def _mixer_b(u, w_in, qn_g, kn_g, rows, need_ctx):
    d = D_MODEL
    dkv = B_KV_HEADS * HEAD
    base = jnp.asarray(_deinterleave_perm(1), dtype=jnp.int32)
    w_q = w_in[:, :d][:, _head_perm(N_HEADS, 1)].astype(BF16)
    w_k = w_in[:, d:d + dkv][:, _head_perm(B_KV_HEADS, 1)].astype(BF16)
    rope = _rope_tables(rows.seq, HEAD, 1, rows.bm)
    qh = _proj(u, w_q, 0, d, rows, mode="rope", rope=rope, scale=HEAD ** -0.5 * LOG2E,
               gain=qn_g.astype(F32)[base].reshape(1, HEAD))
    kh = _proj(u, w_k, 0, dkv, rows, mode="rope", rope=rope, gain=kn_g.astype(F32)[base].reshape(1, HEAD), bn=dkv)
    vh = _proj(u, w_in.astype(BF16), d + dkv, dkv, rows, bn=dkv)
    o_l = _attention(qh, kh, vh, rows, kv_group=B_GROUP)
    if not need_ctx:
        return o_l
    return o_l, _attention(qh, kh, vh, rows, kv_group=B_GROUP, ctx_queries=True)


def _mixer_c(u, w_in, conv_w, conv_b, a_log, dt_bias, d_skip, norm_g, rows):
    w = w_in.astype(BF16)
    n_dt = w_in.shape[1] - C_DI - C_CONV_DIM
    z = _proj(u, w, 0, C_DI, rows)
    xbc = _proj(u, w, C_DI, C_CONV_DIM, rows)
    dt = _proj(u, w, C_DI + C_CONV_DIM, n_dt, rows, out_dtype=F32, bn=n_dt)
    xbc = _conv_silu(xbc, conv_w.astype(F32), conv_b.astype(F32), rows)
    y_fwd, y_bwd = _ssd(xbc, dt.T, a_log, dt_bias, d_skip, rows)
    return _gate_norm(y_fwd, y_bwd, z, norm_g)


def _mixer_d(u, w_in, rpb, rows, need_ctx):
    d = D_MODEL
    w = w_in.astype(BF16)
    qh = _proj(u, w, 0, d, rows, scale=HEAD ** -0.5 * LOG2E)
    kh = _proj(u, w, d, d, rows)
    vh = _proj(u, w, 2 * d, d, rows)
    o_l = _neighbourhood(qh, kh, vh, rpb, rows)
    if not need_ctx:
        return o_l
    return o_l, _attention(qh, kh, vh, rows, kv_group=1, ctx_queries=True)


def _row_block(seq, ctx_total, target=1024):
    bm = math.gcd(seq, ctx_total)
    while bm > target and bm % 2 == 0:
        bm //= 2
    return bm


def kernel(x, c, ctx, c_ctx, ada_w, ada_b, ln_g, ln_b, mlp_w1, mlp_w2, a_w_in, a_lambda, a_sub_g, a_w_out, b_w_in, b_qn_g, b_kn_g, b_w_out, c_w_in, c_conv_w, c_conv_b, c_A_log, c_dt_bias, c_D, c_norm_g, c_w_out, d_w_in, d_rpb, d_w_out):
    batch, seq, d = x.shape
    n_ctx = ctx.shape[1]
    depth = ada_w.shape[0]
    assert d == D_MODEL and batch < SUBLANES and seq % GRID_W == 0
    rows = _Rows(batch, seq, n_ctx, _row_block(seq, batch * n_ctx))

    cvec = jnp.zeros((SUBLANES, d), F32).at[:batch].set(c.astype(F32)).at[batch].set(c_ctx.astype(F32))
    mods = _ada_mods(cvec, ada_w.astype(F32), ada_b.astype(F32))
    x_all = jnp.concatenate([x.reshape(batch * seq, d), ctx.reshape(batch * n_ctx, d)], axis=0).astype(F32)
    u = _modulate(x_all, mods, 0, rows)

    for i in range(depth):
        mixer, j = i % N_MIXERS, i // N_MIXERS
        last = i == depth - 1
        if mixer == 0:
            o = _mixer_a(u, a_w_in[j], a_lambda[j], a_sub_g[j], rows, 0.8 - 0.6 * math.exp(-0.3 * i), not last)
            w_out = a_w_out[j]
        elif mixer == 1:
            o = _mixer_b(u, b_w_in[j], b_qn_g[j], b_kn_g[j], rows, not last)
            w_out = b_w_out[j]
        elif mixer == 2:
            o = _mixer_c(u, c_w_in[j], c_conv_w[j], c_conv_b[j], c_A_log[j], c_dt_bias[j], c_D[j], c_norm_g[j], rows)
            w_out = c_w_out[j]
        else:
            o = _mixer_d(u, d_w_in[j], d_rpb[j], rows, not last)
            w_out = d_w_out[j]
        n_rows = rows.n_lat if last else rows.n_all
        x_all, u = _mm_ln(o, w_out.astype(BF16), x_all, mods, i, 2, ln_g[i, 0].astype(F32), ln_b[i, 0].astype(F32),
                          rows, next_mod=(i, 3, 4), n_rows=n_rows)
        hidden = _proj(u, mlp_w1[i].astype(BF16), 0, D_FF, rows, mode="relu2", n_rows=n_rows)
        x_all, u = _mm_ln(hidden, mlp_w2[i].astype(BF16), x_all, mods, i, 5, ln_g[i, 1].astype(F32),
                          ln_b[i, 1].astype(F32), rows, next_mod=None if last else (i + 1, 0, 1), n_rows=n_rows)
    return x_all.reshape(batch, seq, d).astype(x.dtype)
```

```python
import functools
import math

import jax
import jax.numpy as jnp
from jax import lax
from jax.experimental import pallas as pl
from jax.experimental.pallas import tpu as pltpu

F32 = jnp.float32
BF16 = jnp.bfloat16

D_MODEL = 2048
DEPTH = 4
N_MIXERS = 4
GRID_W = 64
D_FF = 4 * D_MODEL
ROPE_BASE = 10000.0
EPS = 1e-6
DN_ALPHA = (2.0 * DEPTH) ** 0.25
N_HEADS = D_MODEL // 128
HEAD = 128
A_DK = 64
B_KV_HEADS = 4
B_GROUP = N_HEADS // B_KV_HEADS
C_DI = 2 * D_MODEL
C_HD = 64
C_GROUPS = 8
C_HPG = (C_DI // C_HD) // C_GROUPS
C_STATE = 128
C_CONV = 5
C_CHUNK = 128
C_CONV_DIM = C_DI + 2 * C_GROUPS * C_STATE
NA_ROWS = 8
NA_COLS = 16
NA_QROWS = 8
NA_WROWS = NA_QROWS + NA_ROWS
MASKED = -1e30
LOG2E = math.log2(math.e)
SSD_GROUPS_PER_STEP = 4
ATTN_SUB_ROWS = 256
ATTN_KEY_CHUNK = 1024

V7X_VMEM_BYTES = 64 * 1024 * 1024
VMEM_LIMIT = V7X_VMEM_BYTES - 8 * 1024 * 1024
LANES = 128
SUBLANES = 8


def _params(*sem):
    return pltpu.CompilerParams(dimension_semantics=sem, vmem_limit_bytes=VMEM_LIMIT)


def _dot(a, b):
    return jnp.dot(a, b, preferred_element_type=F32)


def _dot_nt(a, b, precision=None):
    return lax.dot_general(a, b, (((1,), (1,)), ((), ())), preferred_element_type=F32, precision=precision)


def _silu(x):
    return x / (1.0 + jnp.exp(-x))


def _ada_kernel(c_ref, w_ref, b_ref, o_ref):
    sc = _silu(c_ref[...]).astype(BF16)
    o_ref[...] = _dot(sc, w_ref[...].astype(BF16)) + b_ref[...]


def _ada_mods(cvec, ada_w, ada_b):
    depth, d, n = ada_w.shape
    bn = 1024
    out = pl.pallas_call(
        _ada_kernel,
        grid=(depth, n // bn),
        in_specs=[
            pl.BlockSpec((SUBLANES, d), lambda l, j: (0, 0)),
            pl.BlockSpec((None, d, bn), lambda l, j: (l, 0, j)),
            pl.BlockSpec((None, 1, bn), lambda l, j: (l, 0, j)),
        ],
        out_specs=pl.BlockSpec((None, SUBLANES, bn), lambda l, j: (l, 0, j)),
        out_shape=jax.ShapeDtypeStruct((depth, SUBLANES, n), F32),
        compiler_params=_params("parallel", "parallel"),
        name="ada_mods",
    )(cvec, ada_w, ada_b.reshape(depth, 1, n))
    return out.reshape(depth, SUBLANES, 6, 1, d)


class _Rows:
    def __init__(self, batch, seq, ctx, bm):
        self.batch, self.seq, self.ctx, self.bm = batch, seq, ctx, bm
        self.n_lat = batch * seq
        self.n_all = batch * (seq + ctx)
        assert seq % bm == 0 and (batch * ctx) % bm == 0

    def mod_row(self, i):
        return jnp.minimum((i * self.bm) // self.seq, self.batch)

    def mod_spec(self, mods, layer, k, grid_pos):
        d = mods.shape[-1]
        return pl.BlockSpec((None, None, None, 1, d),
                            lambda *g: (layer, self.mod_row(g[grid_pos]), k, 0, 0))


def _modulate_kernel(x_ref, sh_ref, sc_ref, u_ref):
    u_ref[...] = (x_ref[...] * (1.0 + sc_ref[...]) + sh_ref[...]).astype(u_ref.dtype)


def _modulate(x_all, mods, layer, rows):
    m, d = x_all.shape
    bm = rows.bm
    return pl.pallas_call(
        _modulate_kernel,
        grid=(m // bm,),
        in_specs=[pl.BlockSpec((bm, d), lambda i: (i, 0)),
                  rows.mod_spec(mods, layer, 0, 0), rows.mod_spec(mods, layer, 1, 0)],
        out_specs=pl.BlockSpec((bm, d), lambda i: (i, 0)),
        out_shape=jax.ShapeDtypeStruct((m, d), BF16),
        compiler_params=_params("parallel"),
        name="modulate",
    )(x_all, mods, mods)


def _norm_rope(x, cos, sin, gain, scale):
    if gain is not None:
        x = x * lax.rsqrt(jnp.mean(x * x, axis=1, keepdims=True) + EPS) * gain
    x = x * cos + pltpu.roll(x, HEAD // 2, 1) * sin
    return x * scale if scale != 1.0 else x


def _proj_kernel(*refs, mode, scale, norm):
    if mode == "rope":
        if norm:
            u_ref, w_ref, cos_ref, sin_ref, g_ref, o_ref = refs
        else:
            u_ref, w_ref, cos_ref, sin_ref, o_ref = refs
    else:
        u_ref, w_ref, o_ref = refs
    acc = _dot(u_ref[...], w_ref[...])
    if mode == "relu2":
        r = jnp.maximum(acc, 0.0)
        o_ref[...] = (r * r).astype(o_ref.dtype)
    elif mode == "plain":
        if scale != 1.0:
            acc = acc * scale
        o_ref[...] = acc.astype(o_ref.dtype)
    else:
        cos = cos_ref[...]
        sin = sin_ref[...]
        gain = g_ref[...] if norm else None
        for c in range(0, acc.shape[1], HEAD):
            o_ref[:, c:c + HEAD] = _norm_rope(acc[:, c:c + HEAD], cos, sin, gain, scale).astype(o_ref.dtype)


def _proj(u, w, col0, ncols, rows, *, mode="plain", scale=1.0, rope=None, gain=None,
          out_dtype=BF16, bn=1024, n_rows=None):
    m = u.shape[0] if n_rows is None else n_rows
    k = u.shape[1]
    bm = rows.bm
    bn = min(bn, ncols)
    assert ncols % bn == 0 and col0 % bn == 0 and m % bm == 0
    cb0 = col0 // bn
    in_specs = [pl.BlockSpec((bm, k), lambda j, i: (i, 0)),
                pl.BlockSpec((k, bn), lambda j, i: (0, cb0 + j))]
    args = [u, w]
    if mode == "rope":
        cos, sin = rope
        n_seq = rows.seq // bm
        n_lat = rows.n_lat // bm

        def tab(j, i):
            return (jnp.where(i < n_lat, i % n_seq, n_seq), 0)

        in_specs += [pl.BlockSpec((bm, HEAD), tab), pl.BlockSpec((bm, HEAD), tab)]
        args += [cos, sin]
        if gain is not None:
            in_specs.append(pl.BlockSpec((1, HEAD), lambda j, i: (0, 0)))
            args.append(gain)
    return pl.pallas_call(
        functools.partial(_proj_kernel, mode=mode, scale=scale, norm=gain is not None),
        grid=(ncols // bn, m // bm),
        in_specs=in_specs,
        out_specs=pl.BlockSpec((bm, bn), lambda j, i: (i, j)),
        out_shape=jax.ShapeDtypeStruct((m, ncols), out_dtype),
        compiler_params=_params("parallel", "parallel"),
        name="proj_" + mode,
    )(*args)


def _mm_ln_kernel(*refs, nk, emit_u, n_lat_blocks):
    refs = list(refs)
    if n_lat_blocks is None:
        a_ref = refs.pop(0)
        load_a = lambda: a_ref[...]
    else:
        a_lat_ref, a_ctx_ref = refs.pop(0), refs.pop(0)
        load_a = lambda: jnp.where(pl.program_id(0) < n_lat_blocks, a_lat_ref[...], a_ctx_ref[...])

    if emit_u:
        w_ref, x_ref, gate_ref, g_ref, b_ref, sh_ref, sc_ref, xo_ref, uo_ref = refs[:9]
        rest = refs[9:]
    else:
        w_ref, x_ref, gate_ref, g_ref, b_ref, xo_ref = refs[:6]
        rest = refs[6:]

    def finish(y):
        v = DN_ALPHA * x_ref[...] + gate_ref[...] * y
        mu = jnp.mean(v, axis=1, keepdims=True)
        vc = v - mu
        var = jnp.mean(vc * vc, axis=1, keepdims=True)
        o = vc * lax.rsqrt(var + EPS) * g_ref[...] + b_ref[...]
        xo_ref[...] = o
        if emit_u:
            uo_ref[...] = (o * (1.0 + sc_ref[...]) + sh_ref[...]).astype(uo_ref.dtype)

    if nk == 1:
        finish(_dot(load_a(), w_ref[...]))
    else:
        acc_ref, = rest
        kk = pl.program_id(1)

        @pl.when(kk == 0)
        def _():
            acc_ref[...] = _dot(load_a(), w_ref[...])

        @pl.when(jnp.logical_and(kk > 0, kk < nk - 1))
        def _():
            acc_ref[...] = _dot(load_a(), w_ref[...]) + acc_ref[...]

        @pl.when(kk == nk - 1)
        def _():
            finish(_dot(load_a(), w_ref[...]) + acc_ref[...])


def _mm_ln(a, w, x_res, mods, layer, gate_k, ln_g, ln_b, rows, *, next_mod=None, n_rows=None):
    split = isinstance(a, tuple)
    a_parts = a if split else (a,)
    m = sum(p.shape[0] for p in a_parts) if n_rows is None else n_rows
    k = a_parts[0].shape[1]
    d = w.shape[1]
    if k <= 2048:
        bm, bk = 256, k
    else:
        bm, bk = 512, 2048
    bm = min(bm, rows.bm)
    nk = k // bk
    sub = _Rows(rows.batch, rows.seq, rows.ctx, bm)
    emit_u = next_mod is not None
    vec = pl.BlockSpec((1, d), lambda i, kk: (0, 0))
    if split:
        n_lat_blocks = a_parts[0].shape[0] // bm
        assert a_parts[0].shape[0] % bm == 0 and a_parts[1].shape[0] % bm == 0
        in_specs = [pl.BlockSpec((bm, bk), lambda i, kk: (jnp.minimum(i, n_lat_blocks - 1), kk)),
                    pl.BlockSpec((bm, bk), lambda i, kk: (jnp.maximum(i - n_lat_blocks, 0), kk))]
    else:
        n_lat_blocks = None
        in_specs = [pl.BlockSpec((bm, bk), lambda i, kk: (i, kk))]
    in_specs += [pl.BlockSpec((bk, d), lambda i, kk: (kk, 0)),
                 pl.BlockSpec((bm, d), lambda i, kk: (i, 0)),
                 sub.mod_spec(mods, layer, gate_k, 0), vec, vec]
    args = [*a_parts, w, x_res, mods, ln_g.reshape(1, d), ln_b.reshape(1, d)]
    out_specs = [pl.BlockSpec((bm, d), lambda i, kk: (i, 0))]
    out_shape = [jax.ShapeDtypeStruct((m, d), F32)]
    if emit_u:
        nl, ksh, ksc = next_mod
        in_specs += [sub.mod_spec(mods, nl, ksh, 0), sub.mod_spec(mods, nl, ksc, 0)]
        args += [mods, mods]
        out_specs.append(pl.BlockSpec((bm, d), lambda i, kk: (i, 0)))
        out_shape.append(jax.ShapeDtypeStruct((m, d), BF16))
    res = pl.pallas_call(
        functools.partial(_mm_ln_kernel, nk=nk, emit_u=emit_u, n_lat_blocks=n_lat_blocks),
        grid=(m // bm, nk),
        in_specs=in_specs,
        out_specs=out_specs,
        out_shape=out_shape,
        scratch_shapes=[pltpu.VMEM((bm, d), F32)] if nk > 1 else [],
        compiler_params=_params("parallel", "arbitrary"),
        name="mm_ln",
    )(*args)
    return (res[0], res[1]) if emit_u else (res[0], None)


def _softmax_pipeline(n_items, n_rows, segments, score, value, s_ref):
    def lane_groups(t):
        return [t[:, g * LANES:(g + 1) * LANES] for g in range(t.shape[1] // LANES)]

    def pass1(i, seg, mxv):
        s = score(i, seg)
        col = seg[-1]
        s_ref[i % 2, :, col:col + s.shape[1]] = s
        for part in lane_groups(s):
            mxv = part if mxv is None else jnp.maximum(mxv, part)
        return mxv

    def pass2(i, seg, mx, acc):
        v = value(seg)
        col = seg[-1]
        p = jnp.exp2((s_ref[i % 2, :, col:col + v.shape[0]] - mx).astype(BF16))
        return acc + _dot(p, jnp.concatenate([v, jnp.ones_like(v)], axis=1))

    outs = []
    mxv = None
    for seg in segments:
        mxv = pass1(0, seg, mxv)
    for i in range(n_items):
        mx = jnp.max(mxv, axis=1, keepdims=True)
        acc, mxv = jnp.zeros((n_rows, 2 * LANES), F32), None
        for seg in segments:
            if i + 1 < n_items:
                mxv = pass1(i + 1, seg, mxv)
            acc = pass2(i, seg, mx, acc)
        outs.append(acc[:, :LANES] / acc[:, LANES:])
    return outs


def _attn_kernel(*refs, n_maps, with_lat, lam_init, key_chunk, sub_rows):
    refs = list(refs)
    if n_maps == 2:
        lam_ref = refs.pop(0)
    q_ref, kc_ref, vc_ref = refs[:3]
    refs = refs[3:]
    if with_lat:
        kl_ref, vl_ref = refs[:2]
        refs = refs[2:]
    if n_maps == 2:
        g_ref = refs.pop(0)
    o_ref, s_ref = refs

    segments = [(kc_ref, vc_ref, 0, kc_ref.shape[0], 0)]
    if with_lat:
        n_ctx = kc_ref.shape[0]
        segments += [(kl_ref, vl_ref, c * key_chunk, key_chunk, n_ctx + c * key_chunk)
                     for c in range(kl_ref.shape[0] // key_chunk)]
    n_sub = q_ref.shape[0] // sub_rows
    items = []
    for r in range(n_sub):
        q = q_ref[r * sub_rows:(r + 1) * sub_rows, :]
        for mi in range(n_maps):
            if n_maps == 2:
                lane = lax.broadcasted_iota(jnp.int32, q.shape, 1)
                items.append(jnp.where((lane // 32) % 2 == mi, q, jnp.zeros_like(q)))
            else:
                items.append(q)

    def score(i, seg):
        k_ref, _, off, n, _ = seg
        return _dot_nt(items[i], k_ref[off:off + n, :])

    def value(seg):
        _, v_ref, off, n, _ = seg
        return v_ref[off:off + n, :]

    outs = _softmax_pipeline(len(items), sub_rows, segments, score, value, s_ref)

    if n_maps == 2:
        lv = lam_ref[...]
        lam = (jnp.exp(jnp.sum(lv[0:1] * lv[1:2], axis=1, keepdims=True))
               - jnp.exp(jnp.sum(lv[2:3] * lv[3:4], axis=1, keepdims=True)) + lam_init)
    for r in range(n_sub):
        if n_maps == 2:
            o = outs[2 * r] - lam * outs[2 * r + 1]
            o = o * lax.rsqrt(jnp.mean(o * o, axis=1, keepdims=True) + EPS) * g_ref[...] * (1.0 - lam_init)
        else:
            o = outs[r]
        o_ref[r * sub_rows:(r + 1) * sub_rows, :] = o.astype(o_ref.dtype)


def _attention(q, k, v, rows, *, kv_group, n_maps=1, lam_vec=None, sub_g=None, lam_init=0.0,
               ctx_queries=False, key_chunk=ATTN_KEY_CHUNK, sub_rows=ATTN_SUB_ROWS, blocks_per_step=4):
    batch, seq, ctx = rows.batch, rows.seq, rows.ctx
    key_chunk = min(key_chunk, seq)
    while seq % key_chunk:
        key_chunk //= 2
    bq = sub_rows * blocks_per_step
    while seq % bq:
        bq //= 2
    n_heads = q.shape[1] // HEAD
    lat_blocks = rows.n_lat // ctx
    if ctx_queries:
        bq = ctx
        nq = 1
        q_map = lambda b, h, j: (lat_blocks + b, h)
        m_out = batch * ctx
    else:
        nq = seq // bq
        q_map = lambda b, h, j: (b * nq + j, h)
        m_out = rows.n_lat
    kc_map = lambda b, h, j: (lat_blocks + b, h // kv_group)
    in_specs, args = [], []
    if n_maps == 2:
        in_specs.append(pl.BlockSpec(lam_vec.shape, lambda b, h, j: (0, 0)))
        args.append(lam_vec)
    in_specs += [pl.BlockSpec((bq, HEAD), q_map),
                 pl.BlockSpec((ctx, HEAD), kc_map), pl.BlockSpec((ctx, HEAD), kc_map)]
    args += [q, k, v]
    if not ctx_queries:
        kl_map = lambda b, h, j: (b, h // kv_group)
        in_specs += [pl.BlockSpec((seq, HEAD), kl_map), pl.BlockSpec((seq, HEAD), kl_map)]
        args += [k, v]
    if n_maps == 2:
        in_specs.append(pl.BlockSpec((1, HEAD), lambda b, h, j: (0, 0)))
        args.append(sub_g)
    o_map = (lambda b, h, j: (b, h)) if ctx_queries else (lambda b, h, j: (b * nq + j, h))
    scratch = [pltpu.VMEM((2, min(sub_rows, bq), ctx if ctx_queries else ctx + seq), F32)]
    return pl.pallas_call(
        functools.partial(_attn_kernel, n_maps=n_maps, with_lat=not ctx_queries, lam_init=lam_init,
                          key_chunk=min(key_chunk, seq), sub_rows=min(sub_rows, bq)),
        grid=(batch, n_heads, nq),
        in_specs=in_specs,
        out_specs=pl.BlockSpec((bq, HEAD), o_map),
        out_shape=jax.ShapeDtypeStruct((m_out, n_heads * HEAD), BF16),
        scratch_shapes=scratch,
        compiler_params=_params("parallel", "parallel", "arbitrary"),
        name="attn_ctx" if ctx_queries else "attn_lat",
    )(*args)


def _na_window_start(kind, a):
    if kind == 0:
        return max(a - NA_ROWS // 2, 0), 0
    if kind == 1:
        return a, NA_ROWS // 2
    return min(a + NA_ROWS // 2, NA_WROWS - NA_ROWS), NA_WROWS - NA_QROWS


def _na_build_bias(tiles_ref, bias_ref, kind):
    w = GRID_W
    lane = lax.broadcasted_iota(jnp.int32, (w, 2 * w), 1)
    masked = jnp.full((w, 2 * w), MASKED, F32)
    for a in range(NA_QROWS):
        lo, off = _na_window_start(kind, a)
        for c in range(0, NA_WROWS, 2):
            ok0 = lo <= c < lo + NA_ROWS
            ok1 = lo <= c + 1 < lo + NA_ROWS
            if ok0 or ok1:
                i0 = c - a - off + NA_ROWS - 1
                assert 0 <= i0 <= 2 * NA_ROWS - 2
                tile = tiles_ref[i0]
                if not ok1:
                    tile = jnp.where(lane < w, tile, MASKED)
                elif not ok0:
                    tile = jnp.where(lane >= w, tile, MASKED)
            else:
                tile = masked
            bias_ref[kind, a * w:(a + 1) * w, c * w:(c + 2) * w] = tile


def _na_kernel(q_ref, k_ref, v_ref, kc_ref, vc_ref, tiles_ref, o_ref, bias_ref, s_ref, *, n_rows,
               key_chunk=ATTN_KEY_CHUNK, sub_rows=ATTN_SUB_ROWS):
    b = pl.program_id(1)
    j = pl.program_id(2)
    nb = n_rows // NA_QROWS
    wlen = NA_WROWS * GRID_W

    for kind, jj in ((0, 0), (1, 1), (2, nb - 1)):
        @pl.when(jnp.logical_and(b == 0, j == jj))
        def _():
            _na_build_bias(tiles_ref, bias_ref, kind)

    kind = jnp.where(j == 0, 0, jnp.where(j == nb - 1, 2, 1))
    start = pl.multiple_of(jnp.clip(NA_QROWS * j - NA_ROWS // 2, 0, n_rows - NA_WROWS) * GRID_W, 4 * GRID_W)
    n_ctx = kc_ref.shape[0]
    n_items = q_ref.shape[0] // sub_rows
    segments = [(False, 0, n_ctx, 0)]
    segments += [(True, c * key_chunk, key_chunk, n_ctx + c * key_chunk) for c in range(wlen // key_chunk)]

    def score(i, seg):
        window, off, n, _ = seg
        q = q_ref[i * sub_rows:(i + 1) * sub_rows, :]
        if not window:
            return _dot_nt(q, kc_ref[...])
        k = k_ref[pl.ds(start + off, n), :]
        return _dot_nt(q, k) + bias_ref[kind, i * sub_rows:(i + 1) * sub_rows, off:off + n]

    def value(seg):
        window, off, n, _ = seg
        return v_ref[pl.ds(start + off, n), :] if window else vc_ref[...]

    outs = _softmax_pipeline(n_items, sub_rows, segments, score, value, s_ref)
    for i, o in enumerate(outs):
        o_ref[i * sub_rows:(i + 1) * sub_rows, :] = o.astype(o_ref.dtype)


def _na_tiles_kernel(rpb_ref, o_ref):
    n = o_ref.shape[1]
    shift = GRID_W.bit_length() - 1
    jj = lax.broadcasted_iota(jnp.int32, (LANES, n), 0)
    pos = lax.broadcasted_iota(jnp.int32, (LANES, n), 1)
    rel = (pos & (GRID_W - 1)) - (pos >> shift) + (NA_COLS - 1)
    onehot = (rel == jj).astype(F32)
    vals = jnp.dot(rpb_ref[...], onehot, preferred_element_type=F32, precision=lax.Precision.HIGHEST)
    pos1 = lax.broadcasted_iota(jnp.int32, (1, n), 1)
    qc = pos1 >> shift
    kc = pos1 & (GRID_W - 1)
    d = kc - jnp.clip(qc - NA_COLS // 2, 0, GRID_W - NA_COLS)
    inside = jnp.logical_and(d >= 0, d < NA_COLS)
    o_ref[...] = jnp.where(inside, vals * LOG2E, MASKED)


def _na_tiles(rpb):
    n_heads, n_ri, n_ci = rpb.shape
    assert n_ri == 2 * NA_ROWS - 1 and n_ci == 2 * NA_COLS - 1 and n_ci <= LANES and n_ri < 2 * SUBLANES
    padded = jnp.pad(rpb.astype(F32), ((0, 0), (0, 2 * SUBLANES - n_ri), (0, LANES - n_ci)))
    n = GRID_W * GRID_W
    flat = pl.pallas_call(
        _na_tiles_kernel,
        grid=(n_heads,),
        in_specs=[pl.BlockSpec((None, 2 * SUBLANES, LANES), lambda h: (h, 0, 0))],
        out_specs=pl.BlockSpec((None, 2 * SUBLANES, n), lambda h: (h, 0, 0)),
        out_shape=jax.ShapeDtypeStruct((n_heads, 2 * SUBLANES, n), F32),
        compiler_params=_params("parallel"),
        name="na_tiles",
    )(padded)
    t = flat.reshape(n_heads, 2 * SUBLANES, GRID_W, GRID_W)
    return jnp.concatenate([t[:, :n_ri], t[:, 1:n_ri + 1]], axis=-1)


def _neighbourhood(q, k, v, rpb, rows):
    batch, seq, ctx = rows.batch, rows.seq, rows.ctx
    n_rows = seq // GRID_W
    assert n_rows % NA_QROWS == 0 and n_rows >= NA_WROWS
    nb = n_rows // NA_QROWS
    bq = NA_QROWS * GRID_W
    wlen = NA_WROWS * GRID_W
    lat_blocks = rows.n_lat // ctx
    tiles = _na_tiles(rpb)
    return pl.pallas_call(
        functools.partial(_na_kernel, n_rows=n_rows),
        grid=(N_HEADS, batch, nb),
        in_specs=[pl.BlockSpec((bq, HEAD), lambda h, b, j: (b * nb + j, h)),
                  pl.BlockSpec((seq, HEAD), lambda h, b, j: (b, h)),
                  pl.BlockSpec((seq, HEAD), lambda h, b, j: (b, h)),
                  pl.BlockSpec((ctx, HEAD), lambda h, b, j: (lat_blocks + b, h)),
                  pl.BlockSpec((ctx, HEAD), lambda h, b, j: (lat_blocks + b, h)),
                  pl.BlockSpec((None,) + tiles.shape[1:], lambda h, b, j: (h, 0, 0, 0))],
        out_specs=pl.BlockSpec((bq, HEAD), lambda h, b, j: (b * nb + j, h)),
        out_shape=jax.ShapeDtypeStruct((rows.n_lat, N_HEADS * HEAD), BF16),
        scratch_shapes=[pltpu.VMEM((3, bq, wlen), F32), pltpu.VMEM((2, ATTN_SUB_ROWS, ctx + wlen), F32)],
        compiler_params=_params("arbitrary", "arbitrary", "arbitrary"),
        name="na_attn",
    )(q, k, v, k, v, tiles)


def _conv_kernel(prev_ref, cur_ref, next_ref, w_ref, b_ref, o_ref, ext_ref, *, blocks_per_seq, n_lat_blocks):
    i = pl.program_id(1)
    bm = cur_ref.shape[0]
    pos = i % blocks_per_seq
    is_lat = i < n_lat_blocks
    first = jnp.logical_or(jnp.logical_not(is_lat), pos == 0)
    last = jnp.logical_or(jnp.logical_not(is_lat), pos == blocks_per_seq - 1)
    halo_p = prev_ref[...].astype(F32)
    halo_n = next_ref[...].astype(F32)
    ext_ref[0:SUBLANES, :] = jnp.where(first, jnp.zeros_like(halo_p), halo_p)
    ext_ref[SUBLANES:SUBLANES + bm, :] = cur_ref[...].astype(F32)
    ext_ref[SUBLANES + bm:2 * SUBLANES + bm, :] = jnp.where(last, jnp.zeros_like(halo_n), halo_n)
    w = w_ref[...]
    acc = jnp.zeros(o_ref.shape, F32) + b_ref[...]
    for t in range(C_CONV):
        off = SUBLANES - C_CONV // 2 + t
        acc = acc + ext_ref[off:off + bm, :] * w[t:t + 1, :]
    o_ref[...] = _silu(acc).astype(o_ref.dtype)


def _conv_silu(xbc, conv_w, conv_b, rows, bc=2048):
    m, n = xbc.shape
    bm = rows.ctx
    assert rows.seq % bm == 0 and bm % SUBLANES == 0
    hb = bm // SUBLANES
    n8 = m // SUBLANES
    return pl.pallas_call(
        functools.partial(_conv_kernel, blocks_per_seq=rows.seq // bm, n_lat_blocks=rows.n_lat // bm),
        grid=(n // bc, m // bm),
        in_specs=[pl.BlockSpec((SUBLANES, bc), lambda j, i: (jnp.maximum(i * hb - 1, 0), j)),
                  pl.BlockSpec((bm, bc), lambda j, i: (i, j)),
                  pl.BlockSpec((SUBLANES, bc), lambda j, i: (jnp.minimum((i + 1) * hb, n8 - 1), j)),
                  pl.BlockSpec((C_CONV, bc), lambda j, i: (0, j)),
                  pl.BlockSpec((1, bc), lambda j, i: (0, j))],
        out_specs=pl.BlockSpec((bm, bc), lambda j, i: (i, j)),
        out_shape=jax.ShapeDtypeStruct((m, n), BF16),
        scratch_shapes=[pltpu.VMEM((bm + 2 * SUBLANES, bc), F32)],
        compiler_params=_params("parallel", "arbitrary"),
        name="conv_silu",
    )(xbc, xbc, xbc, conv_w, conv_b.reshape(1, n))


def _ssd_chunk(x_ref, b_ref, c_ref, dt_ref, alog_ref, dtb_ref, dsk_ref, y_ref, h_ref, forward):
    q = C_CHUNK
    half = q // 2
    assert q == 2 * C_HD and C_HPG % 2 == 0
    row = lax.broadcasted_iota(jnp.int32, (q, q), 0)
    col = lax.broadcasted_iota(jnp.int32, (q, q), 1)
    before = (col <= row) if forward else (col >= row)

    def thrice(mask):
        m = jnp.where(mask, 1.0, 0.0).astype(BF16)
        return jnp.concatenate([m, m, m], axis=1)

    def pieces(v):
        hi = v.astype(BF16).astype(F32)
        r = v - hi
        mid = r.astype(BF16).astype(F32)
        return hi, mid, r - mid

    def per_channel(v):
        rep = lambda t: jnp.concatenate([jnp.broadcast_to(t[h:h + 1, :], (C_HD, q)) for h in range(C_HPG)], axis=0)
        return jnp.concatenate([rep(t).astype(BF16) for t in pieces(v)], axis=1)

    dt_row = dt_ref[...] + dtb_ref[...]
    dt_row = jnp.maximum(dt_row, 0.0) + jnp.log1p(jnp.exp(-jnp.abs(dt_row)))
    dta_row = dt_row * (-jnp.exp(alog_ref[...]))
    yield
    before3 = thrice(before)
    cs_e = _dot_nt(before3, per_channel(dta_row))
    dt_e = _dot_nt(thrice(row == col), per_channel(dt_row))
    pad = jnp.zeros_like(dta_row)
    dta16 = jnp.concatenate([jnp.concatenate([t, pad], axis=0).astype(BF16) for t in pieces(dta_row)], axis=1)
    cs_row = _dot_nt(dta16, before3)[:C_HPG]
    end_e = cs_e[q - 1:q, :] if forward else cs_e[0:1, :]
    yield

    x = x_ref[...].astype(F32)
    xdt = x * dt_e
    bc = b_ref[...]
    cc = c_ref[...]
    cb = _dot_nt(cc, bc)
    h_old = h_ref[...]
    y = _dot(cc, h_old.astype(BF16)) * jnp.exp(cs_e) + dsk_ref[...] * x
    yield

    lane = lax.broadcasted_iota(jnp.int32, (1, q), 1)
    same_half = (row < half) == (col < half)
    parts = []
    for h0 in range(0, C_HPG, 2):
        cp = cs_e[:, h0 * C_HD:(h0 + 2) * C_HD]
        r1 = jnp.where(lane < half, cs_row[h0:h0 + 1, :], cs_row[h0 + 1:h0 + 2, :])
        r2 = jnp.where(lane < half, cs_row[h0 + 1:h0 + 2, :], cs_row[h0:h0 + 1, :])
        u1 = cb * jnp.exp(jnp.where(before, cp - r1, MASKED))
        u2 = cb * jnp.exp(jnp.where(before, pltpu.roll(cp, half, 1) - r2, MASKED))
        xp = xdt[:, h0 * C_HD:(h0 + 2) * C_HD]
        rhs = jnp.concatenate([jnp.where(same_half, xp, 0.0), jnp.where(same_half, 0.0, xp)], axis=0)
        parts.append(_dot(jnp.concatenate([u1, u2], axis=1).astype(BF16), rhs.astype(BF16)))
        yield
    y_ref[...] = (y + jnp.concatenate(parts, axis=1)).astype(y_ref.dtype)

    bct = bc.astype(F32).T.astype(BF16)
    h_ref[...] = h_old * jnp.exp(end_e) + _dot(bct, (xdt * jnp.exp(end_e - cs_e)).astype(BF16))


def _ssd_kernel(*refs):
    y_refs, h_ref = refs[14:16], refs[16]
    hp = C_HPG * C_HD

    @pl.when(pl.program_id(2) == 0)
    def _():
        h_ref[...] = jnp.zeros_like(h_ref)

    chains = []
    for d in range(2):
        x_ref, b_ref, c_ref, dt_ref, alog_ref, dtb_ref, dsk_ref = refs[7 * d:7 * d + 7]
        for g in range(SSD_GROUPS_PER_STEP):
            wide = (slice(None), pl.ds(g * hp, hp))
            state = (slice(None), pl.ds(g * C_STATE, C_STATE))
            head = (pl.ds(g * C_HPG, C_HPG), slice(None))
            chains.append(_ssd_chunk(x_ref.at[wide], b_ref.at[state], c_ref.at[state], dt_ref.at[head],
                                     alog_ref.at[head], dtb_ref.at[head], dsk_ref.at[wide], y_refs[d].at[wide],
                                     h_ref.at[d, g], d == 0))
    while chains:
        chains = [c for c in chains if next(c, StopIteration) is not StopIteration]


def _ssd(xbc, dt_t, a_log, dt_bias, d_skip, rows):
    batch, seq, ctx = rows.batch, rows.seq, rows.ctx
    m = xbc.shape[0]
    q = C_CHUNK
    nc_ctx = ctx // q
    nc_lat = seq // q
    nc = nc_ctx + nc_lat
    lat_chunks = rows.n_lat // q
    hp = C_HPG * C_HD

    def chunk(d):
        def index(b, s):
            c_ctx = s if d == 0 else nc_ctx - 1 - s
            c_lat = s - nc_ctx if d == 0 else nc - 1 - s
            return jnp.where(s < nc_ctx, lat_chunks + b * nc_ctx + c_ctx, b * nc_lat + c_lat)
        return index

    gps = SSD_GROUPS_PER_STEP
    ng = C_GROUPS // gps
    n_xb = C_DI // (gps * C_STATE)
    n_bb = ng
    heads = 2 * C_GROUPS * C_HPG
    col = lambda t: t.astype(F32).reshape(heads, 1)
    dsk_e = jnp.repeat(d_skip.astype(F32).reshape(2, 1, C_GROUPS * C_HPG), C_HD, axis=2)
    in_specs, args, out_specs = [], [], []
    for d in range(2):
        ch = chunk(d)
        hspec = pl.BlockSpec((gps * C_HPG, 1), lambda b, g, s, d=d: (d * ng + g, 0))
        in_specs += [pl.BlockSpec((q, gps * hp), lambda b, g, s, ch=ch: (ch(b, s), g)),
                     pl.BlockSpec((q, gps * C_STATE), lambda b, g, s, ch=ch: (ch(b, s), n_xb + g)),
                     pl.BlockSpec((q, gps * C_STATE), lambda b, g, s, ch=ch: (ch(b, s), n_xb + n_bb + g)),
                     pl.BlockSpec((gps * C_HPG, q), lambda b, g, s, ch=ch, d=d: (d * ng + g, ch(b, s))),
                     hspec, hspec,
                     pl.BlockSpec((None, 1, gps * hp), lambda b, g, s, d=d: (d, 0, g))]
        args += [xbc, xbc, xbc, dt_t, col(a_log), col(dt_bias), dsk_e]
        out_specs.append(pl.BlockSpec((q, gps * hp), lambda b, g, s, ch=ch: (ch(b, s), g)))
    return pl.pallas_call(
        _ssd_kernel,
        grid=(batch, ng, nc),
        in_specs=in_specs,
        out_specs=out_specs,
        out_shape=[jax.ShapeDtypeStruct((m, C_DI), BF16)] * 2,
        scratch_shapes=[pltpu.VMEM((2, gps, C_STATE, hp), F32)],
        compiler_params=_params("parallel", "parallel", "arbitrary"),
        name="ssd_scan",
    )(*args)


def _gate_norm_kernel(yf_ref, yb_ref, z_ref, g_ref, o_ref):
    v = (yf_ref[...].astype(F32) + yb_ref[...].astype(F32)) * _silu(z_ref[...].astype(F32))
    o_ref[...] = (v * lax.rsqrt(jnp.mean(v * v, axis=1, keepdims=True) + EPS) * g_ref[...]).astype(o_ref.dtype)


def _gate_norm(y_fwd, y_bwd, z, norm_g, bm=256):
    m, n = z.shape
    blk = pl.BlockSpec((bm, n), lambda i: (i, 0))
    return pl.pallas_call(
        _gate_norm_kernel,
        grid=(m // bm,),
        in_specs=[blk, blk, blk, pl.BlockSpec((1, n), lambda i: (0, 0))],
        out_specs=blk,
        out_shape=jax.ShapeDtypeStruct((m, n), BF16),
        compiler_params=_params("parallel"),
        name="gate_norm",
    )(y_fwd, y_bwd, z, norm_g.astype(F32).reshape(1, n))


def _rope_tables(seq, dim, n_copies, bm):
    t = jnp.arange(seq, dtype=jnp.int32)
    row = (t // GRID_W).astype(F32)
    colp = (t % GRID_W).astype(F32)
    n_pairs = dim // 4
    inv = ROPE_BASE ** (-jnp.arange(n_pairs, dtype=F32) / n_pairs)
    ang = jnp.concatenate([row[:, None] * inv, colp[:, None] * inv], axis=-1)
    cos = jnp.tile(jnp.cos(ang), (1, 2 * n_copies))
    sin = jnp.tile(jnp.sin(ang), (1, n_copies))
    sin = jnp.concatenate([-sin, sin], axis=-1)
    cos = jnp.concatenate([cos, jnp.ones((bm, HEAD), F32)], axis=0)
    sin = jnp.concatenate([sin, jnp.zeros((bm, HEAD), F32)], axis=0)
    return cos, sin


def _deinterleave_perm(n_maps):
    dim = HEAD // n_maps
    perm = []
    for e in range(2):
        for mi in range(n_maps):
            for p in range(dim // 2):
                perm.append(mi * dim + 2 * p + e)
    return perm


def _head_perm(n_heads, n_maps):
    base = _deinterleave_perm(n_maps)
    return jnp.asarray([h * HEAD + c for h in range(n_heads) for c in base], dtype=jnp.int32)


def _mixer_a(u, w_in, lam_vec, sub_g, rows, lam_init, need_ctx):
    d = D_MODEL
    perm = _head_perm(N_HEADS, 2)
    w_q = w_in[:, :d][:, perm].astype(BF16)
    w_k = w_in[:, d:2 * d][:, perm].astype(BF16)
    rope = _rope_tables(rows.seq, A_DK, 2, rows.bm)
    qh = _proj(u, w_q, 0, d, rows, mode="rope", rope=rope, scale=A_DK ** -0.5 * LOG2E)
    kh = _proj(u, w_k, 0, d, rows, mode="rope", rope=rope)
    vh = _proj(u, w_in.astype(BF16), 2 * d, d, rows)
    kw = dict(kv_group=1, n_maps=2, lam_vec=lam_vec.astype(F32), sub_g=sub_g.astype(F32).reshape(1, HEAD),
              lam_init=lam_init)
    o_l = _attention(qh, kh, vh, rows, **kw)
    if not need_ctx:
        return o_l
    return o_l, _attention(qh, kh, vh, rows, ctx_queries=True, **kw)


def _mixer_b(u, w_in, qn_g, kn_g, rows, need_ctx):
    d = D_MODEL
    dkv = B_KV_HEADS * HEAD
    base = jnp.asarray(_deinterleave_perm(1), dtype=jnp.int32)
    w_q = w_in[:, :d][:, _head_perm(N_HEADS, 1)].astype(BF16)
    w_k = w_in[:, d:d + dkv][:, _head_perm(B_KV_HEADS, 1)].astype(BF16)
    rope = _rope_tables(rows.seq, HEAD, 1, rows.bm)
    qh = _proj(u, w_q, 0, d, rows, mode="rope", rope=rope, scale=HEAD ** -0.5 * LOG2E,
               gain=qn_g.astype(F32)[base].reshape(1, HEAD))
    kh = _proj(u, w_k, 0, dkv, rows, mode="rope", rope=rope, gain=kn_g.astype(F32)[base].reshape(1, HEAD), bn=dkv)
    vh = _proj(u, w_in.astype(BF16), d + dkv, dkv, rows, bn=dkv)
    o_l = _attention(qh, kh, vh, rows, kv_group=B_GROUP)
    if not need_ctx:
        return o_l
    return o_l, _attention(qh, kh, vh, rows, kv_group=B_GROUP, ctx_queries=True)


def _mixer_c(u, w_in, conv_w, conv_b, a_log, dt_bias, d_skip, norm_g, rows):
    w = w_in.astype(BF16)
    n_dt = w_in.shape[1] - C_DI - C_CONV_DIM
    z = _proj(u, w, 0, C_DI, rows)
    xbc = _proj(u, w, C_DI, C_CONV_DIM, rows)
    dt = _proj(u, w, C_DI + C_CONV_DIM, n_dt, rows, out_dtype=F32, bn=n_dt)
    xbc = _conv_silu(xbc, conv_w.astype(F32), conv_b.astype(F32), rows)
    y_fwd, y_bwd = _ssd(xbc, dt.T, a_log, dt_bias, d_skip, rows)
    return _gate_norm(y_fwd, y_bwd, z, norm_g)


def _mixer_d(u, w_in, rpb, rows, need_ctx):
    d = D_MODEL
    w = w_in.astype(BF16)
    qh = _proj(u, w, 0, d, rows, scale=HEAD ** -0.5 * LOG2E)
    kh = _proj(u, w, d, d, rows)
    vh = _proj(u, w, 2 * d, d, rows)
    o_l = _neighbourhood(qh, kh, vh, rpb, rows)
    if not need_ctx:
        return o_l
    return o_l, _attention(qh, kh, vh, rows, kv_group=1, ctx_queries=True)


def _row_block(seq, ctx_total, target=1024):
    bm = math.gcd(seq, ctx_total)
    while bm > target and bm % 2 == 0:
        bm //= 2
    return bm


def kernel(x, c, ctx, c_ctx, ada_w, ada_b, ln_g, ln_b, mlp_w1, mlp_w2, a_w_in, a_lambda, a_sub_g, a_w_out, b_w_in, b_qn_g, b_kn_g, b_w_out, c_w_in, c_conv_w, c_conv_b, c_A_log, c_dt_bias, c_D, c_norm_g, c_w_out, d_w_in, d_rpb, d_w_out):
    batch, seq, d = x.shape
    n_ctx = ctx.shape[1]
    depth = ada_w.shape[0]
    assert d == D_MODEL and batch < SUBLANES and seq % GRID_W == 0
    rows = _Rows(batch, seq, n_ctx, _row_block(seq, batch * n_ctx))

    cvec = jnp.zeros((SUBLANES, d), F32).at[:batch].set(c.astype(F32)).at[batch].set(c_ctx.astype(F32))
    mods = _ada_mods(cvec, ada_w.astype(F32), ada_b.astype(F32))
    x_all = jnp.concatenate([x.reshape(batch * seq, d), ctx.reshape(batch * n_ctx, d)], axis=0).astype(F32)
    u = _modulate(x_all, mods, 0, rows)

    for i in range(depth):
        mixer, j = i % N_MIXERS, i // N_MIXERS
        last = i == depth - 1
        if mixer == 0:
            o = _mixer_a(u, a_w_in[j], a_lambda[j], a_sub_g[j], rows, 0.8 - 0.6 * math.exp(-0.3 * i), not last)
            w_out = a_w_out[j]
        elif mixer == 1:
            o = _mixer_b(u, b_w_in[j], b_qn_g[j], b_kn_g[j], rows, not last)
            w_out = b_w_out[j]
        elif mixer == 2:
            o = _mixer_c(u, c_w_in[j], c_conv_w[j], c_conv_b[j], c_A_log[j], c_dt_bias[j], c_D[j], c_norm_g[j], rows)
            w_out = c_w_out[j]
        else:
            o = _mixer_d(u, d_w_in[j], d_rpb[j], rows, not last)
            w_out = d_w_out[j]
        n_rows = rows.n_lat if last else rows.n_all
        x_all, u = _mm_ln(o, w_out.astype(BF16), x_all, mods, i, 2, ln_g[i, 0].astype(F32), ln_b[i, 0].astype(F32),
                          rows, next_mod=(i, 3, 4), n_rows=n_rows)
        hidden = _proj(u, mlp_w1[i].astype(BF16), 0, D_FF, rows, mode="relu2", n_rows=n_rows)
        x_all, u = _mm_ln(hidden, mlp_w2[i].astype(BF16), x_all, mods, i, 5, ln_g[i, 1].astype(F32),
                          ln_b[i, 1].astype(F32), rows, next_mod=None if last else (i + 1, 0, 1), n_rows=n_rows)
    return x_all.reshape(batch, seq, d).astype(x.dtype)
```

```python
import functools
import math

import jax
import jax.numpy as jnp
from jax import lax
from jax.experimental import pallas as pl
from jax.experimental.pallas import tpu as pltpu

F32 = jnp.float32
BF16 = jnp.bfloat16

D_MODEL = 2048
DEPTH = 4
N_MIXERS = 4
GRID_W = 64
D_FF = 4 * D_MODEL
ROPE_BASE = 10000.0
EPS = 1e-6
DN_ALPHA = (2.0 * DEPTH) ** 0.25
N_HEADS = D_MODEL // 128
HEAD = 128
A_DK = 64
B_KV_HEADS = 4
B_GROUP = N_HEADS // B_KV_HEADS
C_DI = 2 * D_MODEL
C_HD = 64
C_GROUPS = 8
C_HPG = (C_DI // C_HD) // C_GROUPS
C_STATE = 128
C_CONV = 5
C_CHUNK = 128
C_CONV_DIM = C_DI + 2 * C_GROUPS * C_STATE
NA_ROWS = 8
NA_COLS = 16
NA_QROWS = 8
NA_WROWS = NA_QROWS + NA_ROWS
MASKED = -1e30
LOG2E = math.log2(math.e)
SSD_GROUPS_PER_STEP = 8
ATTN_SUB_ROWS = 256
ATTN_KEY_CHUNK = 1024

V7X_VMEM_BYTES = 64 * 1024 * 1024
VMEM_LIMIT = V7X_VMEM_BYTES - 8 * 1024 * 1024
LANES = 128
SUBLANES = 8


def _params(*sem):
    return pltpu.CompilerParams(dimension_semantics=sem, vmem_limit_bytes=VMEM_LIMIT)


def _dot(a, b):
    return jnp.dot(a, b, preferred_element_type=F32)


def _dot_nt(a, b, precision=None):
    return lax.dot_general(a, b, (((1,), (1,)), ((), ())), preferred_element_type=F32, precision=precision)


def _silu(x):
    return x / (1.0 + jnp.exp(-x))


def _ada_kernel(c_ref, w_ref, b_ref, o_ref):
    sc = _silu(c_ref[...]).astype(BF16)
    o_ref[...] = _dot(sc, w_ref[...].astype(BF16)) + b_ref[...]


def _ada_mods(cvec, ada_w, ada_b):
    depth, d, n = ada_w.shape
    bn = 1024
    out = pl.pallas_call(
        _ada_kernel,
        grid=(depth, n // bn),
        in_specs=[
            pl.BlockSpec((SUBLANES, d), lambda l, j: (0, 0)),
            pl.BlockSpec((None, d, bn), lambda l, j: (l, 0, j)),
            pl.BlockSpec((None, 1, bn), lambda l, j: (l, 0, j)),
        ],
        out_specs=pl.BlockSpec((None, SUBLANES, bn), lambda l, j: (l, 0, j)),
        out_shape=jax.ShapeDtypeStruct((depth, SUBLANES, n), F32),
        compiler_params=_params("parallel", "parallel"),
        name="ada_mods",
    )(cvec, ada_w, ada_b.reshape(depth, 1, n))
    return out.reshape(depth, SUBLANES, 6, 1, d)


class _Rows:
    def __init__(self, batch, seq, ctx, bm):
        self.batch, self.seq, self.ctx, self.bm = batch, seq, ctx, bm
        self.n_lat = batch * seq
        self.n_all = batch * (seq + ctx)
        assert seq % bm == 0 and (batch * ctx) % bm == 0

    def mod_row(self, i):
        return jnp.minimum((i * self.bm) // self.seq, self.batch)

    def mod_spec(self, mods, layer, k, grid_pos):
        d = mods.shape[-1]
        return pl.BlockSpec((None, None, None, 1, d),
                            lambda *g: (layer, self.mod_row(g[grid_pos]), k, 0, 0))


def _modulate_kernel(x_ref, sh_ref, sc_ref, u_ref):
    u_ref[...] = (x_ref[...] * (1.0 + sc_ref[...]) + sh_ref[...]).astype(u_ref.dtype)


def _modulate(x_all, mods, layer, rows):
    m, d = x_all.shape
    bm = rows.bm
    return pl.pallas_call(
        _modulate_kernel,
        grid=(m // bm,),
        in_specs=[pl.BlockSpec((bm, d), lambda i: (i, 0)),
                  rows.mod_spec(mods, layer, 0, 0), rows.mod_spec(mods, layer, 1, 0)],
        out_specs=pl.BlockSpec((bm, d), lambda i: (i, 0)),
        out_shape=jax.ShapeDtypeStruct((m, d), BF16),
        compiler_params=_params("parallel"),
        name="modulate",
    )(x_all, mods, mods)


def _norm_rope(x, cos, sin, gain, scale):
    if gain is not None:
        x = x * lax.rsqrt(jnp.mean(x * x, axis=1, keepdims=True) + EPS) * gain
    x = x * cos + pltpu.roll(x, HEAD // 2, 1) * sin
    return x * scale if scale != 1.0 else x


def _proj_kernel(*refs, mode, scale, norm):
    if mode == "rope":
        if norm:
            u_ref, w_ref, cos_ref, sin_ref, g_ref, o_ref = refs
        else:
            u_ref, w_ref, cos_ref, sin_ref, o_ref = refs
    else:
        u_ref, w_ref, o_ref = refs
    acc = _dot(u_ref[...], w_ref[...])
    if mode == "relu2":
        r = jnp.maximum(acc, 0.0)
        o_ref[...] = (r * r).astype(o_ref.dtype)
    elif mode == "plain":
        if scale != 1.0:
            acc = acc * scale
        o_ref[...] = acc.astype(o_ref.dtype)
    else:
        cos = cos_ref[...]
        sin = sin_ref[...]
        gain = g_ref[...] if norm else None
        for c in range(0, acc.shape[1], HEAD):
            o_ref[:, c:c + HEAD] = _norm_rope(acc[:, c:c + HEAD], cos, sin, gain, scale).astype(o_ref.dtype)


def _proj(u, w, col0, ncols, rows, *, mode="plain", scale=1.0, rope=None, gain=None,
          out_dtype=BF16, bn=1024, n_rows=None):
    m = u.shape[0] if n_rows is None else n_rows
    k = u.shape[1]
    bm = rows.bm
    bn = min(bn, ncols)
    assert ncols % bn == 0 and col0 % bn == 0 and m % bm == 0
    cb0 = col0 // bn
    in_specs = [pl.BlockSpec((bm, k), lambda j, i: (i, 0)),
                pl.BlockSpec((k, bn), lambda j, i: (0, cb0 + j))]
    args = [u, w]
    if mode == "rope":
        cos, sin = rope
        n_seq = rows.seq // bm
        n_lat = rows.n_lat // bm

        def tab(j, i):
            return (jnp.where(i < n_lat, i % n_seq, n_seq), 0)

        in_specs += [pl.BlockSpec((bm, HEAD), tab), pl.BlockSpec((bm, HEAD), tab)]
        args += [cos, sin]
        if gain is not None:
            in_specs.append(pl.BlockSpec((1, HEAD), lambda j, i: (0, 0)))
            args.append(gain)
    return pl.pallas_call(
        functools.partial(_proj_kernel, mode=mode, scale=scale, norm=gain is not None),
        grid=(ncols // bn, m // bm),
        in_specs=in_specs,
        out_specs=pl.BlockSpec((bm, bn), lambda j, i: (i, j)),
        out_shape=jax.ShapeDtypeStruct((m, ncols), out_dtype),
        compiler_params=_params("parallel", "parallel"),
        name="proj_" + mode,
    )(*args)


def _mm_ln_kernel(*refs, nk, emit_u, n_lat_blocks):
    refs = list(refs)
    if n_lat_blocks is None:
        a_ref = refs.pop(0)
        load_a = lambda: a_ref[...]
    else:
        a_lat_ref, a_ctx_ref = refs.pop(0), refs.pop(0)
        load_a = lambda: jnp.where(pl.program_id(0) < n_lat_blocks, a_lat_ref[...], a_ctx_ref[...])

    if emit_u:
        w_ref, x_ref, gate_ref, g_ref, b_ref, sh_ref, sc_ref, xo_ref, uo_ref = refs[:9]
        rest = refs[9:]
    else:
        w_ref, x_ref, gate_ref, g_ref, b_ref, xo_ref = refs[:6]
        rest = refs[6:]

    def finish(y):
        v = DN_ALPHA * x_ref[...] + gate_ref[...] * y
        mu = jnp.mean(v, axis=1, keepdims=True)
        vc = v - mu
        var = jnp.mean(vc * vc, axis=1, keepdims=True)
        o = vc * lax.rsqrt(var + EPS) * g_ref[...] + b_ref[...]
        xo_ref[...] = o
        if emit_u:
            uo_ref[...] = (o * (1.0 + sc_ref[...]) + sh_ref[...]).astype(uo_ref.dtype)

    if nk == 1:
        finish(_dot(load_a(), w_ref[...]))
    else:
        acc_ref, = rest
        kk = pl.program_id(1)

        @pl.when(kk == 0)
        def _():
            acc_ref[...] = _dot(load_a(), w_ref[...])

        @pl.when(jnp.logical_and(kk > 0, kk < nk - 1))
        def _():
            acc_ref[...] = _dot(load_a(), w_ref[...]) + acc_ref[...]

        @pl.when(kk == nk - 1)
        def _():
            finish(_dot(load_a(), w_ref[...]) + acc_ref[...])


def _mm_ln(a, w, x_res, mods, layer, gate_k, ln_g, ln_b, rows, *, next_mod=None, n_rows=None):
    split = isinstance(a, tuple)
    a_parts = a if split else (a,)
    m = sum(p.shape[0] for p in a_parts) if n_rows is None else n_rows
    k = a_parts[0].shape[1]
    d = w.shape[1]
    if k <= 2048:
        bm, bk = 256, k
    else:
        bm, bk = 512, 2048
    bm = min(bm, rows.bm)
    nk = k // bk
    sub = _Rows(rows.batch, rows.seq, rows.ctx, bm)
    emit_u = next_mod is not None
    vec = pl.BlockSpec((1, d), lambda i, kk: (0, 0))
    if split:
        n_lat_blocks = a_parts[0].shape[0] // bm
        assert a_parts[0].shape[0] % bm == 0 and a_parts[1].shape[0] % bm == 0
        in_specs = [pl.BlockSpec((bm, bk), lambda i, kk: (jnp.minimum(i, n_lat_blocks - 1), kk)),
                    pl.BlockSpec((bm, bk), lambda i, kk: (jnp.maximum(i - n_lat_blocks, 0), kk))]
    else:
        n_lat_blocks = None
        in_specs = [pl.BlockSpec((bm, bk), lambda i, kk: (i, kk))]
    in_specs += [pl.BlockSpec((bk, d), lambda i, kk: (kk, 0)),
                 pl.BlockSpec((bm, d), lambda i, kk: (i, 0)),
                 sub.mod_spec(mods, layer, gate_k, 0), vec, vec]
    args = [*a_parts, w, x_res, mods, ln_g.reshape(1, d), ln_b.reshape(1, d)]
    out_specs = [pl.BlockSpec((bm, d), lambda i, kk: (i, 0))]
    out_shape = [jax.ShapeDtypeStruct((m, d), F32)]
    if emit_u:
        nl, ksh, ksc = next_mod
        in_specs += [sub.mod_spec(mods, nl, ksh, 0), sub.mod_spec(mods, nl, ksc, 0)]
        args += [mods, mods]
        out_specs.append(pl.BlockSpec((bm, d), lambda i, kk: (i, 0)))
        out_shape.append(jax.ShapeDtypeStruct((m, d), BF16))
    res = pl.pallas_call(
        functools.partial(_mm_ln_kernel, nk=nk, emit_u=emit_u, n_lat_blocks=n_lat_blocks),
        grid=(m // bm, nk),
        in_specs=in_specs,
        out_specs=out_specs,
        out_shape=out_shape,
        scratch_shapes=[pltpu.VMEM((bm, d), F32)] if nk > 1 else [],
        compiler_params=_params("parallel", "arbitrary"),
        name="mm_ln",
    )(*args)
    return (res[0], res[1]) if emit_u else (res[0], None)


def _softmax_pipeline(n_items, n_rows, segments, score, value, s_ref):
    def lane_groups(t):
        return [t[:, g * LANES:(g + 1) * LANES] for g in range(t.shape[1] // LANES)]

    def pass1(i, seg, mxv):
        s = score(i, seg)
        col = seg[-1]
        s_ref[i % 2, :, col:col + s.shape[1]] = s
        for part in lane_groups(s):
            mxv = part if mxv is None else jnp.maximum(mxv, part)
        return mxv

    def pass2(i, seg, mx, acc):
        v = value(i, seg)
        col = seg[-1]
        p = jnp.exp2((s_ref[i % 2, :, col:col + v.shape[0]] - mx).astype(BF16))
        return acc + _dot(p, jnp.concatenate([v, jnp.ones_like(v)], axis=1))

    outs = []
    mxv = None
    for seg in segments:
        mxv = pass1(0, seg, mxv)
    for i in range(n_items):
        mx = jnp.max(mxv, axis=1, keepdims=True)
        acc, mxv = jnp.zeros((n_rows, 2 * LANES), F32), None
        for seg in segments:
            if i + 1 < n_items:
                mxv = pass1(i + 1, seg, mxv)
            acc = pass2(i, seg, mx, acc)
        outs.append(acc[:, :LANES] / acc[:, LANES:])
    return outs


def _attn_kernel(*refs, n_maps, with_lat, lam_init, key_chunk, sub_rows):
    refs = list(refs)
    if n_maps == 2:
        lam_ref = refs.pop(0)
    q_ref, kc_ref, vc_ref = refs[:3]
    refs = refs[3:]
    if with_lat:
        kl_ref, vl_ref = refs[:2]
        refs = refs[2:]
    if n_maps == 2:
        g_ref = refs.pop(0)
    o_ref, s_ref = refs

    segments = [(kc_ref, vc_ref, 0, kc_ref.shape[0], 0)]
    if with_lat:
        n_ctx = kc_ref.shape[0]
        segments += [(kl_ref, vl_ref, c * key_chunk, key_chunk, n_ctx + c * key_chunk)
                     for c in range(kl_ref.shape[0] // key_chunk)]
    n_sub = q_ref.shape[0] // sub_rows
    items = []
    for r in range(n_sub):
        q = q_ref[r * sub_rows:(r + 1) * sub_rows, :]
        for mi in range(n_maps):
            if n_maps == 2:
                lane = lax.broadcasted_iota(jnp.int32, q.shape, 1)
                items.append(jnp.where((lane // 32) % 2 == mi, q, jnp.zeros_like(q)))
            else:
                items.append(q)

    def score(i, seg):
        k_ref, _, off, n, _ = seg
        return _dot_nt(items[i], k_ref[off:off + n, :])

    def value(i, seg):
        _, v_ref, off, n, _ = seg
        return v_ref[off:off + n, :]

    outs = _softmax_pipeline(len(items), sub_rows, segments, score, value, s_ref)

    if n_maps == 2:
        lv = lam_ref[...]
        lam = (jnp.exp(jnp.sum(lv[0:1] * lv[1:2], axis=1, keepdims=True))
               - jnp.exp(jnp.sum(lv[2:3] * lv[3:4], axis=1, keepdims=True)) + lam_init)
    for r in range(n_sub):
        if n_maps == 2:
            o = outs[2 * r] - lam * outs[2 * r + 1]
            o = o * lax.rsqrt(jnp.mean(o * o, axis=1, keepdims=True) + EPS) * g_ref[...] * (1.0 - lam_init)
        else:
            o = outs[r]
        o_ref[r * sub_rows:(r + 1) * sub_rows, :] = o.astype(o_ref.dtype)


def _attention(q, k, v, rows, *, kv_group, n_maps=1, lam_vec=None, sub_g=None, lam_init=0.0,
               ctx_queries=False, key_chunk=ATTN_KEY_CHUNK, sub_rows=ATTN_SUB_ROWS, blocks_per_step=4):
    batch, seq, ctx = rows.batch, rows.seq, rows.ctx
    key_chunk = min(key_chunk, seq)
    while seq % key_chunk:
        key_chunk //= 2
    bq = sub_rows * blocks_per_step
    while seq % bq:
        bq //= 2
    n_heads = q.shape[1] // HEAD
    lat_blocks = rows.n_lat // ctx
    if ctx_queries:
        bq = ctx
        nq = 1
        q_map = lambda b, h, j: (lat_blocks + b, h)
        m_out = batch * ctx
    else:
        nq = seq // bq
        q_map = lambda b, h, j: (b * nq + j, h)
        m_out = rows.n_lat
    kc_map = lambda b, h, j: (lat_blocks + b, h // kv_group)
    in_specs, args = [], []
    if n_maps == 2:
        in_specs.append(pl.BlockSpec(lam_vec.shape, lambda b, h, j: (0, 0)))
        args.append(lam_vec)
    in_specs += [pl.BlockSpec((bq, HEAD), q_map),
                 pl.BlockSpec((ctx, HEAD), kc_map), pl.BlockSpec((ctx, HEAD), kc_map)]
    args += [q, k, v]
    if not ctx_queries:
        kl_map = lambda b, h, j: (b, h // kv_group)
        in_specs += [pl.BlockSpec((seq, HEAD), kl_map), pl.BlockSpec((seq, HEAD), kl_map)]
        args += [k, v]
    if n_maps == 2:
        in_specs.append(pl.BlockSpec((1, HEAD), lambda b, h, j: (0, 0)))
        args.append(sub_g)
    o_map = (lambda b, h, j: (b, h)) if ctx_queries else (lambda b, h, j: (b * nq + j, h))
    scratch = [pltpu.VMEM((2, min(sub_rows, bq), ctx if ctx_queries else ctx + seq), F32)]
    return pl.pallas_call(
        functools.partial(_attn_kernel, n_maps=n_maps, with_lat=not ctx_queries, lam_init=lam_init,
                          key_chunk=min(key_chunk, seq), sub_rows=min(sub_rows, bq)),
        grid=(batch, n_heads, nq),
        in_specs=in_specs,
        out_specs=pl.BlockSpec((bq, HEAD), o_map),
        out_shape=jax.ShapeDtypeStruct((m_out, n_heads * HEAD), BF16),
        scratch_shapes=scratch,
        compiler_params=_params("parallel", "parallel", "arbitrary"),
        name="attn_ctx" if ctx_queries else "attn_lat",
    )(*args)


def _na_window_start(kind, a):
    if kind == 0:
        return max(a - NA_ROWS // 2, 0), 0
    if kind == 1:
        return a, NA_ROWS // 2
    return min(a + NA_ROWS // 2, NA_WROWS - NA_ROWS), NA_WROWS - NA_QROWS


def _na_build_bias(tiles_ref, bias_ref, kind):
    w = GRID_W
    lane = lax.broadcasted_iota(jnp.int32, (w, 2 * w), 1)
    masked = jnp.full((w, 2 * w), MASKED, F32)
    for a in range(NA_QROWS):
        lo, off = _na_window_start(kind, a)
        for c in range(0, NA_WROWS, 2):
            ok0 = lo <= c < lo + NA_ROWS
            ok1 = lo <= c + 1 < lo + NA_ROWS
            if ok0 or ok1:
                i0 = c - a - off + NA_ROWS - 1
                assert 0 <= i0 <= 2 * NA_ROWS - 2
                tile = tiles_ref[i0]
                if not ok1:
                    tile = jnp.where(lane < w, tile, MASKED)
                elif not ok0:
                    tile = jnp.where(lane >= w, tile, MASKED)
            else:
                tile = masked
            bias_ref[kind, a * w:(a + 1) * w, c * w:(c + 2) * w] = tile


def _na_kernel(q_ref, k_ref, v_ref, kc_ref, vc_ref, tiles_ref, o_ref, bias_ref, s_ref, *, n_rows, blocks,
               key_chunk=ATTN_KEY_CHUNK, sub_rows=ATTN_SUB_ROWS):
    b = pl.program_id(1)
    step = pl.program_id(2)
    nb = n_rows // NA_QROWS
    wlen = NA_WROWS * GRID_W
    block_rows = NA_QROWS * GRID_W
    per_block = block_rows // sub_rows

    for kind, jj in ((0, 0), (1, 1), (2, nb - 1)):
        @pl.when(jnp.logical_and(b == 0, step == jj // blocks))
        def _():
            _na_build_bias(tiles_ref, bias_ref, kind)

    kinds, starts = [], []
    for t in range(blocks):
        j = step * blocks + t
        kinds.append(jnp.where(j == 0, 0, jnp.where(j == nb - 1, 2, 1)))
        starts.append(pl.multiple_of(
            jnp.clip(NA_QROWS * j - NA_ROWS // 2, 0, n_rows - NA_WROWS) * GRID_W, 4 * GRID_W))
    n_ctx = kc_ref.shape[0]
    n_items = q_ref.shape[0] // sub_rows
    segments = [(False, 0, n_ctx, 0)]
    segments += [(True, c * key_chunk, key_chunk, n_ctx + c * key_chunk) for c in range(wlen // key_chunk)]

    def score(i, seg):
        window, off, n, _ = seg
        q = q_ref[i * sub_rows:(i + 1) * sub_rows, :]
        if not window:
            return _dot_nt(q, kc_ref[...])
        t, r = divmod(i, per_block)
        k = k_ref[pl.ds(starts[t] + off, n), :]
        return _dot_nt(q, k) + bias_ref[kinds[t], r * sub_rows:(r + 1) * sub_rows, off:off + n]

    def value(i, seg):
        window, off, n, _ = seg
        return v_ref[pl.ds(starts[i // per_block] + off, n), :] if window else vc_ref[...]

    outs = _softmax_pipeline(n_items, sub_rows, segments, score, value, s_ref)
    for i, o in enumerate(outs):
        o_ref[i * sub_rows:(i + 1) * sub_rows, :] = o.astype(o_ref.dtype)


def _na_tiles_kernel(rpb_ref, o_ref):
    n = o_ref.shape[1]
    shift = GRID_W.bit_length() - 1
    jj = lax.broadcasted_iota(jnp.int32, (LANES, n), 0)
    pos = lax.broadcasted_iota(jnp.int32, (LANES, n), 1)
    rel = (pos & (GRID_W - 1)) - (pos >> shift) + (NA_COLS - 1)
    onehot = (rel == jj).astype(F32)
    vals = jnp.dot(rpb_ref[...], onehot, preferred_element_type=F32, precision=lax.Precision.HIGHEST)
    pos1 = lax.broadcasted_iota(jnp.int32, (1, n), 1)
    qc = pos1 >> shift
    kc = pos1 & (GRID_W - 1)
    d = kc - jnp.clip(qc - NA_COLS // 2, 0, GRID_W - NA_COLS)
    inside = jnp.logical_and(d >= 0, d < NA_COLS)
    o_ref[...] = jnp.where(inside, vals * LOG2E, MASKED)


def _na_tiles(rpb):
    n_heads, n_ri, n_ci = rpb.shape
    assert n_ri == 2 * NA_ROWS - 1 and n_ci == 2 * NA_COLS - 1 and n_ci <= LANES and n_ri < 2 * SUBLANES
    padded = jnp.pad(rpb.astype(F32), ((0, 0), (0, 2 * SUBLANES - n_ri), (0, LANES - n_ci)))
    n = GRID_W * GRID_W
    flat = pl.pallas_call(
        _na_tiles_kernel,
        grid=(n_heads,),
        in_specs=[pl.BlockSpec((None, 2 * SUBLANES, LANES), lambda h: (h, 0, 0))],
        out_specs=pl.BlockSpec((None, 2 * SUBLANES, n), lambda h: (h, 0, 0)),
        out_shape=jax.ShapeDtypeStruct((n_heads, 2 * SUBLANES, n), F32),
        compiler_params=_params("parallel"),
        name="na_tiles",
    )(padded)
    t = flat.reshape(n_heads, 2 * SUBLANES, GRID_W, GRID_W)
    return jnp.concatenate([t[:, :n_ri], t[:, 1:n_ri + 1]], axis=-1)


def _neighbourhood(q, k, v, rpb, rows):
    batch, seq, ctx = rows.batch, rows.seq, rows.ctx
    n_rows = seq // GRID_W
    assert n_rows % NA_QROWS == 0 and n_rows >= NA_WROWS
    nb = n_rows // NA_QROWS
    bq = NA_QROWS * GRID_W
    wlen = NA_WROWS * GRID_W
    lat_blocks = rows.n_lat // ctx
    tiles = _na_tiles(rpb)
    blocks = 2 if nb % 2 == 0 else 1
    ns = nb // blocks
    return pl.pallas_call(
        functools.partial(_na_kernel, n_rows=n_rows, blocks=blocks),
        grid=(N_HEADS, batch, ns),
        in_specs=[pl.BlockSpec((blocks * bq, HEAD), lambda h, b, j: (b * ns + j, h)),
                  pl.BlockSpec((seq, HEAD), lambda h, b, j: (b, h)),
                  pl.BlockSpec((seq, HEAD), lambda h, b, j: (b, h)),
                  pl.BlockSpec((ctx, HEAD), lambda h, b, j: (lat_blocks + b, h)),
                  pl.BlockSpec((ctx, HEAD), lambda h, b, j: (lat_blocks + b, h)),
                  pl.BlockSpec((None,) + tiles.shape[1:], lambda h, b, j: (h, 0, 0, 0))],
        out_specs=pl.BlockSpec((blocks * bq, HEAD), lambda h, b, j: (b * ns + j, h)),
        out_shape=jax.ShapeDtypeStruct((rows.n_lat, N_HEADS * HEAD), BF16),
        scratch_shapes=[pltpu.VMEM((3, bq, wlen), F32), pltpu.VMEM((2, ATTN_SUB_ROWS, ctx + wlen), F32)],
        compiler_params=_params("arbitrary", "arbitrary", "arbitrary"),
        name="na_attn",
    )(q, k, v, k, v, tiles)


def _conv_kernel(prev_ref, cur_ref, next_ref, w_ref, b_ref, o_ref, ext_ref, *, blocks_per_seq, n_lat_blocks):
    i = pl.program_id(1)
    bm = cur_ref.shape[0]
    pos = i % blocks_per_seq
    is_lat = i < n_lat_blocks
    first = jnp.logical_or(jnp.logical_not(is_lat), pos == 0)
    last = jnp.logical_or(jnp.logical_not(is_lat), pos == blocks_per_seq - 1)
    halo_p = prev_ref[...].astype(F32)
    halo_n = next_ref[...].astype(F32)
    ext_ref[0:SUBLANES, :] = jnp.where(first, jnp.zeros_like(halo_p), halo_p)
    ext_ref[SUBLANES:SUBLANES + bm, :] = cur_ref[...].astype(F32)
    ext_ref[SUBLANES + bm:2 * SUBLANES + bm, :] = jnp.where(last, jnp.zeros_like(halo_n), halo_n)
    w = w_ref[...]
    acc = jnp.zeros(o_ref.shape, F32) + b_ref[...]
    for t in range(C_CONV):
        off = SUBLANES - C_CONV // 2 + t
        acc = acc + ext_ref[off:off + bm, :] * w[t:t + 1, :]
    o_ref[...] = _silu(acc).astype(o_ref.dtype)


def _conv_silu(xbc, conv_w, conv_b, rows, bc=2048):
    m, n = xbc.shape
    bm = rows.ctx
    assert rows.seq % bm == 0 and bm % SUBLANES == 0
    hb = bm // SUBLANES
    n8 = m // SUBLANES
    return pl.pallas_call(
        functools.partial(_conv_kernel, blocks_per_seq=rows.seq // bm, n_lat_blocks=rows.n_lat // bm),
        grid=(n // bc, m // bm),
        in_specs=[pl.BlockSpec((SUBLANES, bc), lambda j, i: (jnp.maximum(i * hb - 1, 0), j)),
                  pl.BlockSpec((bm, bc), lambda j, i: (i, j)),
                  pl.BlockSpec((SUBLANES, bc), lambda j, i: (jnp.minimum((i + 1) * hb, n8 - 1), j)),
                  pl.BlockSpec((C_CONV, bc), lambda j, i: (0, j)),
                  pl.BlockSpec((1, bc), lambda j, i: (0, j))],
        out_specs=pl.BlockSpec((bm, bc), lambda j, i: (i, j)),
        out_shape=jax.ShapeDtypeStruct((m, n), BF16),
        scratch_shapes=[pltpu.VMEM((bm + 2 * SUBLANES, bc), F32)],
        compiler_params=_params("parallel", "arbitrary"),
        name="conv_silu",
    )(xbc, xbc, xbc, conv_w, conv_b.reshape(1, n))


def _ssd_chunk(x_ref, b_ref, c_ref, dt_ref, alog_ref, dtb_ref, dsk_ref, y_ref, h_ref, forward):
    q = C_CHUNK
    half = q // 2
    assert q == 2 * C_HD and C_HPG % 2 == 0
    row = lax.broadcasted_iota(jnp.int32, (q, q), 0)
    col = lax.broadcasted_iota(jnp.int32, (q, q), 1)
    before = (col <= row) if forward else (col >= row)

    def thrice(mask):
        m = jnp.where(mask, 1.0, 0.0).astype(BF16)
        return jnp.concatenate([m, m, m], axis=1)

    def pieces(v):
        hi = v.astype(BF16).astype(F32)
        r = v - hi
        mid = r.astype(BF16).astype(F32)
        return hi, mid, r - mid

    def per_channel(v):
        rep = lambda t: jnp.concatenate([jnp.broadcast_to(t[h:h + 1, :], (C_HD, q)) for h in range(C_HPG)], axis=0)
        return jnp.concatenate([rep(t).astype(BF16) for t in pieces(v)], axis=1)

    dt_row = dt_ref[...] + dtb_ref[...]
    dt_row = jnp.maximum(dt_row, 0.0) + jnp.log1p(jnp.exp(-jnp.abs(dt_row)))
    dta_row = dt_row * (-jnp.exp(alog_ref[...]))
    yield
    before3 = thrice(before)
    cs_e = _dot_nt(before3, per_channel(dta_row))
    dt_e = _dot_nt(thrice(row == col), per_channel(dt_row))
    pad = jnp.zeros_like(dta_row)
    dta16 = jnp.concatenate([jnp.concatenate([t, pad], axis=0).astype(BF16) for t in pieces(dta_row)], axis=1)
    cs_row = _dot_nt(dta16, before3)[:C_HPG]
    end_e = cs_e[q - 1:q, :] if forward else cs_e[0:1, :]
    yield

    x = x_ref[...].astype(F32)
    xdt = x * dt_e
    bc = b_ref[...]
    cc = c_ref[...]
    cb = _dot_nt(cc, bc)
    h_old = h_ref[...]
    y = _dot(cc, h_old.astype(BF16)) * jnp.exp(cs_e) + dsk_ref[...] * x
    yield

    lane = lax.broadcasted_iota(jnp.int32, (1, q), 1)
    same_half = (row < half) == (col < half)
    parts = []
    for h0 in range(0, C_HPG, 2):
        cp = cs_e[:, h0 * C_HD:(h0 + 2) * C_HD]
        r1 = jnp.where(lane < half, cs_row[h0:h0 + 1, :], cs_row[h0 + 1:h0 + 2, :])
        r2 = jnp.where(lane < half, cs_row[h0 + 1:h0 + 2, :], cs_row[h0:h0 + 1, :])
        u1 = cb * jnp.exp(jnp.where(before, cp - r1, MASKED))
        u2 = cb * jnp.exp(jnp.where(before, pltpu.roll(cp, half, 1) - r2, MASKED))
        xp = xdt[:, h0 * C_HD:(h0 + 2) * C_HD]
        rhs = jnp.concatenate([jnp.where(same_half, xp, 0.0), jnp.where(same_half, 0.0, xp)], axis=0)
        parts.append(_dot(jnp.concatenate([u1, u2], axis=1).astype(BF16), rhs.astype(BF16)))
        yield
    y_ref[...] = (y + jnp.concatenate(parts, axis=1)).astype(y_ref.dtype)

    bct = bc.astype(F32).T.astype(BF16)
    h_ref[...] = h_old * jnp.exp(end_e) + _dot(bct, (xdt * jnp.exp(end_e - cs_e)).astype(BF16))


def _ssd_kernel(*refs):
    y_refs, h_ref = refs[14:16], refs[16]
    hp = C_HPG * C_HD

    @pl.when(pl.program_id(2) == 0)
    def _():
        h_ref[...] = jnp.zeros_like(h_ref)

    chains = []
    for d in range(2):
        x_ref, b_ref, c_ref, dt_ref, alog_ref, dtb_ref, dsk_ref = refs[7 * d:7 * d + 7]
        for g in range(SSD_GROUPS_PER_STEP):
            wide = (slice(None), pl.ds(g * hp, hp))
            state = (slice(None), pl.ds(g * C_STATE, C_STATE))
            head = (pl.ds(g * C_HPG, C_HPG), slice(None))
            chains.append(_ssd_chunk(x_ref.at[wide], b_ref.at[state], c_ref.at[state], dt_ref.at[head],
                                     alog_ref.at[head], dtb_ref.at[head], dsk_ref.at[wide], y_refs[d].at[wide],
                                     h_ref.at[d, g], d == 0))
    while chains:
        chains = [c for c in chains if next(c, StopIteration) is not StopIteration]


def _ssd(xbc, dt_t, a_log, dt_bias, d_skip, rows):
    batch, seq, ctx = rows.batch, rows.seq, rows.ctx
    m = xbc.shape[0]
    q = C_CHUNK
    nc_ctx = ctx // q
    nc_lat = seq // q
    nc = nc_ctx + nc_lat
    lat_chunks = rows.n_lat // q
    hp = C_HPG * C_HD

    def chunk(d):
        def index(b, s):
            c_ctx = s if d == 0 else nc_ctx - 1 - s
            c_lat = s - nc_ctx if d == 0 else nc - 1 - s
            return jnp.where(s < nc_ctx, lat_chunks + b * nc_ctx + c_ctx, b * nc_lat + c_lat)
        return index

    gps = SSD_GROUPS_PER_STEP
    ng = C_GROUPS // gps
    n_xb = C_DI // (gps * C_STATE)
    n_bb = ng
    heads = 2 * C_GROUPS * C_HPG
    col = lambda t: t.astype(F32).reshape(heads, 1)
    dsk_e = jnp.repeat(d_skip.astype(F32).reshape(2, 1, C_GROUPS * C_HPG), C_HD, axis=2)
    in_specs, args, out_specs = [], [], []
    for d in range(2):
        ch = chunk(d)
        hspec = pl.BlockSpec((gps * C_HPG, 1), lambda b, g, s, d=d: (d * ng + g, 0))
        in_specs += [pl.BlockSpec((q, gps * hp), lambda b, g, s, ch=ch: (ch(b, s), g)),
                     pl.BlockSpec((q, gps * C_STATE), lambda b, g, s, ch=ch: (ch(b, s), n_xb + g)),
                     pl.BlockSpec((q, gps * C_STATE), lambda b, g, s, ch=ch: (ch(b, s), n_xb + n_bb + g)),
                     pl.BlockSpec((gps * C_HPG, q), lambda b, g, s, ch=ch, d=d: (d * ng + g, ch(b, s))),
                     hspec, hspec,
                     pl.BlockSpec((None, 1, gps * hp), lambda b, g, s, d=d: (d, 0, g))]
        args += [xbc, xbc, xbc, dt_t, col(a_log), col(dt_bias), dsk_e]
        out_specs.append(pl.BlockSpec((q, gps * hp), lambda b, g, s, ch=ch: (ch(b, s), g)))
    return pl.pallas_call(
        _ssd_kernel,
        grid=(batch, ng, nc),
        in_specs=in_specs,
        out_specs=out_specs,
        out_shape=[jax.ShapeDtypeStruct((m, C_DI), BF16)] * 2,
        scratch_shapes=[pltpu.VMEM((2, gps, C_STATE, hp), F32)],
        compiler_params=_params("parallel", "parallel", "arbitrary"),
        name="ssd_scan",
    )(*args)


def _gate_norm_kernel(yf_ref, yb_ref, z_ref, g_ref, o_ref):
    v = (yf_ref[...].astype(F32) + yb_ref[...].astype(F32)) * _silu(z_ref[...].astype(F32))
    o_ref[...] = (v * lax.rsqrt(jnp.mean(v * v, axis=1, keepdims=True) + EPS) * g_ref[...]).astype(o_ref.dtype)


def _gate_norm(y_fwd, y_bwd, z, norm_g, bm=256):
    m, n = z.shape
    blk = pl.BlockSpec((bm, n), lambda i: (i, 0))
    return pl.pallas_call(
        _gate_norm_kernel,
        grid=(m // bm,),
        in_specs=[blk, blk, blk, pl.BlockSpec((1, n), lambda i: (0, 0))],
        out_specs=blk,
        out_shape=jax.ShapeDtypeStruct((m, n), BF16),
        compiler_params=_params("parallel"),
        name="gate_norm",
    )(y_fwd, y_bwd, z, norm_g.astype(F32).reshape(1, n))


def _rope_tables(seq, dim, n_copies, bm):
    t = jnp.arange(seq, dtype=jnp.int32)
    row = (t // GRID_W).astype(F32)
    colp = (t % GRID_W).astype(F32)
    n_pairs = dim // 4
    inv = ROPE_BASE ** (-jnp.arange(n_pairs, dtype=F32) / n_pairs)
    ang = jnp.concatenate([row[:, None] * inv, colp[:, None] * inv], axis=-1)
    cos = jnp.tile(jnp.cos(ang), (1, 2 * n_copies))
    sin = jnp.tile(jnp.sin(ang), (1, n_copies))
    sin = jnp.concatenate([-sin, sin], axis=-1)
    cos = jnp.concatenate([cos, jnp.ones((bm, HEAD), F32)], axis=0)
    sin = jnp.concatenate([sin, jnp.zeros((bm, HEAD), F32)], axis=0)
    return cos, sin


def _deinterleave_perm(n_maps):
    dim = HEAD // n_maps
    perm = []
    for e in range(2):
        for mi in range(n_maps):
            for p in range(dim // 2):
                perm.append(mi * dim + 2 * p + e)
    return perm


def _head_perm(n_heads, n_maps):
    base = _deinterleave_perm(n_maps)
    return jnp.asarray([h * HEAD + c for h in range(n_heads) for c in base], dtype=jnp.int32)


def _mixer_a(u, w_in, lam_vec, sub_g, rows, lam_init, need_ctx):
    d = D_MODEL
    perm = _head_perm(N_HEADS, 2)
    w_q = w_in[:, :d][:, perm].astype(BF16)
    w_k = w_in[:, d:2 * d][:, perm].astype(BF16)
    rope = _rope_tables(rows.seq, A_DK, 2, rows.bm)
    qh = _proj(u, w_q, 0, d, rows, mode="rope", rope=rope, scale=A_DK ** -0.5 * LOG2E)
    kh = _proj(u, w_k, 0, d, rows, mode="rope", rope=rope)
    vh = _proj(u, w_in.astype(BF16), 2 * d, d, rows)
    kw = dict(kv_group=1, n_maps=2, lam_vec=lam_vec.astype(F32), sub_g=sub_g.astype(F32).reshape(1, HEAD),
              lam_init=lam_init)
    o_l = _attention(qh, kh, vh, rows, **kw)
    if not need_ctx:
        return o_l
    return o_l, _attention(qh, kh, vh, rows, ctx_queries=True, **kw)


def _mixer_b(u, w_in, qn_g, kn_g, rows, need_ctx):
    d = D_MODEL
    dkv = B_KV_HEADS * HEAD
    base = jnp.asarray(_deinterleave_perm(1), dtype=jnp.int32)
    w_q = w_in[:, :d][:, _head_perm(N_HEADS, 1)].astype(BF16)
    w_k = w_in[:, d:d + dkv][:, _head_perm(B_KV_HEADS, 1)].astype(BF16)
    rope = _rope_tables(rows.seq, HEAD, 1, rows.bm)
    qh = _proj(u, w_q, 0, d, rows, mode="rope", rope=rope, scale=HEAD ** -0.5 * LOG2E,
               gain=qn_g.astype(F32)[base].reshape(1, HEAD))
    kh = _proj(u, w_k, 0, dkv, rows, mode="rope", rope=rope, gain=kn_g.astype(F32)[base].reshape(1, HEAD), bn=dkv)
    vh = _proj(u, w_in.astype(BF16), d + dkv, dkv, rows, bn=dkv)
    o_l = _attention(qh, kh, vh, rows, kv_group=B_GROUP, blocks_per_step=8)
    if not need_ctx:
        return o_l
    return o_l, _attention(qh, kh, vh, rows, kv_group=B_GROUP, ctx_queries=True)


def _mixer_c(u, w_in, conv_w, conv_b, a_log, dt_bias, d_skip, norm_g, rows):
    w = w_in.astype(BF16)
    n_dt = w_in.shape[1] - C_DI - C_CONV_DIM
    z = _proj(u, w, 0, C_DI, rows)
    xbc = _proj(u, w, C_DI, C_CONV_DIM, rows)
    dt = _proj(u, w, C_DI + C_CONV_DIM, n_dt, rows, out_dtype=F32, bn=n_dt)
    xbc = _conv_silu(xbc, conv_w.astype(F32), conv_b.astype(F32), rows)
    y_fwd, y_bwd = _ssd(xbc, dt.T, a_log, dt_bias, d_skip, rows)
    return _gate_norm(y_fwd, y_bwd, z, norm_g)


def _mixer_d(u, w_in, rpb, rows, need_ctx):
    d = D_MODEL
    w = w_in.astype(BF16)
    qh = _proj(u, w, 0, d, rows, scale=HEAD ** -0.5 * LOG2E)
    kh = _proj(u, w, d, d, rows)
    vh = _proj(u, w, 2 * d, d, rows)
    o_l = _neighbourhood(qh, kh, vh, rpb, rows)
    if not need_ctx:
        return o_l
    return o_l, _attention(qh, kh, vh, rows, kv_group=1, ctx_queries=True)


def _row_block(seq, ctx_total, target=1024):
    bm = math.gcd(seq, ctx_total)
    while bm > target and bm % 2 == 0:
        bm //= 2
    return bm


def kernel(x, c, ctx, c_ctx, ada_w, ada_b, ln_g, ln_b, mlp_w1, mlp_w2, a_w_in, a_lambda, a_sub_g, a_w_out, b_w_in, b_qn_g, b_kn_g, b_w_out, c_w_in, c_conv_w, c_conv_b, c_A_log, c_dt_bias, c_D, c_norm_g, c_w_out, d_w_in, d_rpb, d_w_out):
    batch, seq, d = x.shape
    n_ctx = ctx.shape[1]
    depth = ada_w.shape[0]
    assert d == D_MODEL and batch < SUBLANES and seq % GRID_W == 0
    rows = _Rows(batch, seq, n_ctx, _row_block(seq, batch * n_ctx))

    cvec = jnp.zeros((SUBLANES, d), F32).at[:batch].set(c.astype(F32)).at[batch].set(c_ctx.astype(F32))
    mods = _ada_mods(cvec, ada_w.astype(F32), ada_b.astype(F32))
    x_all = jnp.concatenate([x.reshape(batch * seq, d), ctx.reshape(batch * n_ctx, d)], axis=0).astype(F32)
    u = _modulate(x_all, mods, 0, rows)

    for i in range(depth):
        mixer, j = i % N_MIXERS, i // N_MIXERS
        last = i == depth - 1
        if mixer == 0:
            o = _mixer_a(u, a_w_in[j], a_lambda[j], a_sub_g[j], rows, 0.8 - 0.6 * math.exp(-0.3 * i), not last)
            w_out = a_w_out[j]
        elif mixer == 1:
            o = _mixer_b(u, b_w_in[j], b_qn_g[j], b_kn_g[j], rows, not last)
            w_out = b_w_out[j]
        elif mixer == 2:
            o = _mixer_c(u, c_w_in[j], c_conv_w[j], c_conv_b[j], c_A_log[j], c_dt_bias[j], c_D[j], c_norm_g[j], rows)
            w_out = c_w_out[j]
        else:
            o = _mixer_d(u, d_w_in[j], d_rpb[j], rows, not last)
            w_out = d_w_out[j]
        n_rows = rows.n_lat if last else rows.n_all
        x_all, u = _mm_ln(o, w_out.astype(BF16), x_all, mods, i, 2, ln_g[i, 0].astype(F32), ln_b[i, 0].astype(F32),
                          rows, next_mod=(i, 3, 4), n_rows=n_rows)
        hidden = _proj(u, mlp_w1[i].astype(BF16), 0, D_FF, rows, mode="relu2", n_rows=n_rows)
        x_all, u = _mm_ln(hidden, mlp_w2[i].astype(BF16), x_all, mods, i, 5, ln_g[i, 1].astype(F32),
                          ln_b[i, 1].astype(F32), rows, next_mod=None if last else (i + 1, 0, 1), n_rows=n_rows)
    return x_all.reshape(batch, seq, d).astype(x.dtype)
```

```python
import functools
import math

import jax
import jax.numpy as jnp
from jax import lax
from jax.experimental import pallas as pl
from jax.experimental.pallas import tpu as pltpu

F32 = jnp.float32
BF16 = jnp.bfloat16

D_MODEL = 2048
DEPTH = 4
N_MIXERS = 4
GRID_W = 64
D_FF = 4 * D_MODEL
ROPE_BASE = 10000.0
EPS = 1e-6
DN_ALPHA = (2.0 * DEPTH) ** 0.25
N_HEADS = D_MODEL // 128
HEAD = 128
A_DK = 64
B_KV_HEADS = 4
B_GROUP = N_HEADS // B_KV_HEADS
C_DI = 2 * D_MODEL
C_HD = 64
C_GROUPS = 8
C_HPG = (C_DI // C_HD) // C_GROUPS
C_STATE = 128
C_CONV = 5
C_CHUNK = 128
C_CONV_DIM = C_DI + 2 * C_GROUPS * C_STATE
NA_ROWS = 8
NA_COLS = 16
NA_QROWS = 8
NA_WROWS = NA_QROWS + NA_ROWS
MASKED = -1e30
LOG2E = math.log2(math.e)
SSD_GROUPS_PER_STEP = 8
ATTN_SUB_ROWS = 256
ATTN_KEY_CHUNK = 1024

V7X_VMEM_BYTES = 64 * 1024 * 1024
VMEM_LIMIT = V7X_VMEM_BYTES - 8 * 1024 * 1024
LANES = 128
SUBLANES = 8


def _params(*sem):
    return pltpu.CompilerParams(dimension_semantics=sem, vmem_limit_bytes=VMEM_LIMIT)


def _dot(a, b):
    return jnp.dot(a, b, preferred_element_type=F32)


def _dot_nt(a, b, precision=None):
    return lax.dot_general(a, b, (((1,), (1,)), ((), ())), preferred_element_type=F32, precision=precision)


def _silu(x):
    return x / (1.0 + jnp.exp(-x))


def _ada_kernel(c_ref, w_ref, b_ref, o_ref):
    sc = _silu(c_ref[...]).astype(BF16)
    o_ref[...] = _dot(sc, w_ref[...].astype(BF16)) + b_ref[...]


def _ada_mods(cvec, ada_w, ada_b):
    depth, d, n = ada_w.shape
    bn = 1024
    out = pl.pallas_call(
        _ada_kernel,
        grid=(depth, n // bn),
        in_specs=[
            pl.BlockSpec((SUBLANES, d), lambda l, j: (0, 0)),
            pl.BlockSpec((None, d, bn), lambda l, j: (l, 0, j)),
            pl.BlockSpec((None, 1, bn), lambda l, j: (l, 0, j)),
        ],
        out_specs=pl.BlockSpec((None, SUBLANES, bn), lambda l, j: (l, 0, j)),
        out_shape=jax.ShapeDtypeStruct((depth, SUBLANES, n), F32),
        compiler_params=_params("parallel", "parallel"),
        name="ada_mods",
    )(cvec, ada_w, ada_b.reshape(depth, 1, n))
    return out.reshape(depth, SUBLANES, 6, 1, d)


class _Rows:
    def __init__(self, batch, seq, ctx, bm):
        self.batch, self.seq, self.ctx, self.bm = batch, seq, ctx, bm
        self.n_lat = batch * seq
        self.n_all = batch * (seq + ctx)
        assert seq % bm == 0 and (batch * ctx) % bm == 0

    def mod_row(self, i):
        return jnp.minimum((i * self.bm) // self.seq, self.batch)

    def mod_spec(self, mods, layer, k, grid_pos):
        d = mods.shape[-1]
        return pl.BlockSpec((None, None, None, 1, d),
                            lambda *g: (layer, self.mod_row(g[grid_pos]), k, 0, 0))


def _modulate_kernel(x_ref, sh_ref, sc_ref, u_ref):
    u_ref[...] = (x_ref[...] * (1.0 + sc_ref[...]) + sh_ref[...]).astype(u_ref.dtype)


def _modulate(x_all, mods, layer, rows):
    m, d = x_all.shape
    bm = rows.bm
    return pl.pallas_call(
        _modulate_kernel,
        grid=(m // bm,),
        in_specs=[pl.BlockSpec((bm, d), lambda i: (i, 0)),
                  rows.mod_spec(mods, layer, 0, 0), rows.mod_spec(mods, layer, 1, 0)],
        out_specs=pl.BlockSpec((bm, d), lambda i: (i, 0)),
        out_shape=jax.ShapeDtypeStruct((m, d), BF16),
        compiler_params=_params("parallel"),
        name="modulate",
    )(x_all, mods, mods)


def _norm_rope(x, cos, sin, gain, scale):
    if gain is not None:
        x = x * lax.rsqrt(jnp.mean(x * x, axis=1, keepdims=True) + EPS) * gain
    x = x * cos + pltpu.roll(x, HEAD // 2, 1) * sin
    return x * scale if scale != 1.0 else x


def _proj_kernel(*refs, mode, scale, norm):
    if mode == "rope":
        if norm:
            u_ref, w_ref, cos_ref, sin_ref, g_ref, o_ref = refs
        else:
            u_ref, w_ref, cos_ref, sin_ref, o_ref = refs
    else:
        u_ref, w_ref, o_ref = refs
    acc = _dot(u_ref[...], w_ref[...])
    if mode == "relu2":
        r = jnp.maximum(acc, 0.0)
        o_ref[...] = (r * r).astype(o_ref.dtype)
    elif mode == "plain":
        if scale != 1.0:
            acc = acc * scale
        o_ref[...] = acc.astype(o_ref.dtype)
    else:
        cos = cos_ref[...]
        sin = sin_ref[...]
        gain = g_ref[...] if norm else None
        for c in range(0, acc.shape[1], HEAD):
            o_ref[:, c:c + HEAD] = _norm_rope(acc[:, c:c + HEAD], cos, sin, gain, scale).astype(o_ref.dtype)


def _proj(u, w, col0, ncols, rows, *, mode="plain", scale=1.0, rope=None, gain=None,
          out_dtype=BF16, bn=1024, n_rows=None):
    m = u.shape[0] if n_rows is None else n_rows
    k = u.shape[1]
    bm = rows.bm
    bn = min(bn, ncols)
    assert ncols % bn == 0 and col0 % bn == 0 and m % bm == 0
    cb0 = col0 // bn
    in_specs = [pl.BlockSpec((bm, k), lambda j, i: (i, 0)),
                pl.BlockSpec((k, bn), lambda j, i: (0, cb0 + j))]
    args = [u, w]
    if mode == "rope":
        cos, sin = rope
        n_seq = rows.seq // bm
        n_lat = rows.n_lat // bm

        def tab(j, i):
            return (jnp.where(i < n_lat, i % n_seq, n_seq), 0)

        in_specs += [pl.BlockSpec((bm, HEAD), tab), pl.BlockSpec((bm, HEAD), tab)]
        args += [cos, sin]
        if gain is not None:
            in_specs.append(pl.BlockSpec((1, HEAD), lambda j, i: (0, 0)))
            args.append(gain)
    return pl.pallas_call(
        functools.partial(_proj_kernel, mode=mode, scale=scale, norm=gain is not None),
        grid=(ncols // bn, m // bm),
        in_specs=in_specs,
        out_specs=pl.BlockSpec((bm, bn), lambda j, i: (i, j)),
        out_shape=jax.ShapeDtypeStruct((m, ncols), out_dtype),
        compiler_params=_params("parallel", "parallel"),
        name="proj_" + mode,
    )(*args)


def _mm_ln_kernel(*refs, nk, emit_u, n_lat_blocks):
    refs = list(refs)
    if n_lat_blocks is None:
        a_ref = refs.pop(0)
        load_a = lambda: a_ref[...]
    else:
        a_lat_ref, a_ctx_ref = refs.pop(0), refs.pop(0)
        load_a = lambda: jnp.where(pl.program_id(0) < n_lat_blocks, a_lat_ref[...], a_ctx_ref[...])

    if emit_u:
        w_ref, x_ref, gate_ref, g_ref, b_ref, sh_ref, sc_ref, xo_ref, uo_ref = refs[:9]
        rest = refs[9:]
    else:
        w_ref, x_ref, gate_ref, g_ref, b_ref, xo_ref = refs[:6]
        rest = refs[6:]

    def finish(y):
        v = DN_ALPHA * x_ref[...] + gate_ref[...] * y
        mu = jnp.mean(v, axis=1, keepdims=True)
        vc = v - mu
        var = jnp.mean(vc * vc, axis=1, keepdims=True)
        o = vc * lax.rsqrt(var + EPS) * g_ref[...] + b_ref[...]
        xo_ref[...] = o
        if emit_u:
            uo_ref[...] = (o * (1.0 + sc_ref[...]) + sh_ref[...]).astype(uo_ref.dtype)

    if nk == 1:
        finish(_dot(load_a(), w_ref[...]))
    else:
        acc_ref, = rest
        kk = pl.program_id(1)

        @pl.when(kk == 0)
        def _():
            acc_ref[...] = _dot(load_a(), w_ref[...])

        @pl.when(jnp.logical_and(kk > 0, kk < nk - 1))
        def _():
            acc_ref[...] = _dot(load_a(), w_ref[...]) + acc_ref[...]

        @pl.when(kk == nk - 1)
        def _():
            finish(_dot(load_a(), w_ref[...]) + acc_ref[...])


def _mm_ln(a, w, x_res, mods, layer, gate_k, ln_g, ln_b, rows, *, next_mod=None, n_rows=None):
    split = isinstance(a, tuple)
    a_parts = a if split else (a,)
    m = sum(p.shape[0] for p in a_parts) if n_rows is None else n_rows
    k = a_parts[0].shape[1]
    d = w.shape[1]
    if k <= 2048:
        bm, bk = 512, k
    else:
        bm, bk = 512, 2048
    bm = min(bm, rows.bm)
    nk = k // bk
    sub = _Rows(rows.batch, rows.seq, rows.ctx, bm)
    emit_u = next_mod is not None
    vec = pl.BlockSpec((1, d), lambda i, kk: (0, 0))
    if split:
        n_lat_blocks = a_parts[0].shape[0] // bm
        assert a_parts[0].shape[0] % bm == 0 and a_parts[1].shape[0] % bm == 0
        in_specs = [pl.BlockSpec((bm, bk), lambda i, kk: (jnp.minimum(i, n_lat_blocks - 1), kk)),
                    pl.BlockSpec((bm, bk), lambda i, kk: (jnp.maximum(i - n_lat_blocks, 0), kk))]
    else:
        n_lat_blocks = None
        in_specs = [pl.BlockSpec((bm, bk), lambda i, kk: (i, kk))]
    in_specs += [pl.BlockSpec((bk, d), lambda i, kk: (kk, 0)),
                 pl.BlockSpec((bm, d), lambda i, kk: (i, 0)),
                 sub.mod_spec(mods, layer, gate_k, 0), vec, vec]
    args = [*a_parts, w, x_res, mods, ln_g.reshape(1, d), ln_b.reshape(1, d)]
    out_specs = [pl.BlockSpec((bm, d), lambda i, kk: (i, 0))]
    out_shape = [jax.ShapeDtypeStruct((m, d), F32)]
    if emit_u:
        nl, ksh, ksc = next_mod
        in_specs += [sub.mod_spec(mods, nl, ksh, 0), sub.mod_spec(mods, nl, ksc, 0)]
        args += [mods, mods]
        out_specs.append(pl.BlockSpec((bm, d), lambda i, kk: (i, 0)))
        out_shape.append(jax.ShapeDtypeStruct((m, d), BF16))
    res = pl.pallas_call(
        functools.partial(_mm_ln_kernel, nk=nk, emit_u=emit_u, n_lat_blocks=n_lat_blocks),
        grid=(m // bm, nk),
        in_specs=in_specs,
        out_specs=out_specs,
        out_shape=out_shape,
        scratch_shapes=[pltpu.VMEM((bm, d), F32)] if nk > 1 else [],
        compiler_params=_params("parallel", "arbitrary"),
        name="mm_ln",
    )(*args)
    return (res[0], res[1]) if emit_u else (res[0], None)


def _softmax_pipeline(n_items, n_rows, segments, score, value, s_ref):
    def lane_groups(t):
        return [t[:, g * LANES:(g + 1) * LANES] for g in range(t.shape[1] // LANES)]

    def pass1(i, seg, mxv):
        s = score(i, seg)
        col = seg[-1]
        s_ref[i % 2, :, col:col + s.shape[1]] = s
        for part in lane_groups(s):
            mxv = part if mxv is None else jnp.maximum(mxv, part)
        return mxv

    def pass2(i, seg, mx, acc):
        v = value(i, seg)
        col = seg[-1]
        p = jnp.exp2((s_ref[i % 2, :, col:col + v.shape[0]] - mx).astype(BF16))
        return acc + _dot(p, jnp.concatenate([v, jnp.ones_like(v)], axis=1))

    outs = []
    mxv = None
    for seg in segments:
        mxv = pass1(0, seg, mxv)
    for i in range(n_items):
        mx = jnp.max(mxv, axis=1, keepdims=True)
        acc, mxv = jnp.zeros((n_rows, 2 * LANES), F32), None
        for seg in segments:
            if i + 1 < n_items:
                mxv = pass1(i + 1, seg, mxv)
            acc = pass2(i, seg, mx, acc)
        outs.append(acc[:, :LANES] / acc[:, LANES:])
    return outs


def _attn_kernel(*refs, n_maps, with_lat, lam_init, key_chunk, sub_rows):
    refs = list(refs)
    if n_maps == 2:
        lam_ref = refs.pop(0)
    q_ref, kc_ref, vc_ref = refs[:3]
    refs = refs[3:]
    if with_lat:
        kl_ref, vl_ref = refs[:2]
        refs = refs[2:]
    if n_maps == 2:
        g_ref = refs.pop(0)
    o_ref, s_ref = refs

    segments = [(kc_ref, vc_ref, 0, kc_ref.shape[0], 0)]
    if with_lat:
        n_ctx = kc_ref.shape[0]
        segments += [(kl_ref, vl_ref, c * key_chunk, key_chunk, n_ctx + c * key_chunk)
                     for c in range(kl_ref.shape[0] // key_chunk)]
    n_sub = q_ref.shape[0] // sub_rows
    items = []
    for r in range(n_sub):
        q = q_ref[r * sub_rows:(r + 1) * sub_rows, :]
        for mi in range(n_maps):
            if n_maps == 2:
                lane = lax.broadcasted_iota(jnp.int32, q.shape, 1)
                items.append(jnp.where((lane // 32) % 2 == mi, q, jnp.zeros_like(q)))
            else:
                items.append(q)

    def score(i, seg):
        k_ref, _, off, n, _ = seg
        return _dot_nt(items[i], k_ref[off:off + n, :])

    def value(i, seg):
        _, v_ref, off, n, _ = seg
        return v_ref[off:off + n, :]

    outs = _softmax_pipeline(len(items), sub_rows, segments, score, value, s_ref)

    if n_maps == 2:
        lv = lam_ref[...]
        lam = (jnp.exp(jnp.sum(lv[0:1] * lv[1:2], axis=1, keepdims=True))
               - jnp.exp(jnp.sum(lv[2:3] * lv[3:4], axis=1, keepdims=True)) + lam_init)
    for r in range(n_sub):
        if n_maps == 2:
            o = outs[2 * r] - lam * outs[2 * r + 1]
            o = o * lax.rsqrt(jnp.mean(o * o, axis=1, keepdims=True) + EPS) * g_ref[...] * (1.0 - lam_init)
        else:
            o = outs[r]
        o_ref[r * sub_rows:(r + 1) * sub_rows, :] = o.astype(o_ref.dtype)


def _attention(q, k, v, rows, *, kv_group, n_maps=1, lam_vec=None, sub_g=None, lam_init=0.0,
               ctx_queries=False, key_chunk=ATTN_KEY_CHUNK, sub_rows=ATTN_SUB_ROWS, blocks_per_step=4):
    batch, seq, ctx = rows.batch, rows.seq, rows.ctx
    key_chunk = min(key_chunk, seq)
    while seq % key_chunk:
        key_chunk //= 2
    bq = sub_rows * blocks_per_step
    while seq % bq:
        bq //= 2
    n_heads = q.shape[1] // HEAD
    lat_blocks = rows.n_lat // ctx
    if ctx_queries:
        bq = ctx
        nq = 1
        q_map = lambda b, h, j: (lat_blocks + b, h)
        m_out = batch * ctx
    else:
        nq = seq // bq
        q_map = lambda b, h, j: (b * nq + j, h)
        m_out = rows.n_lat
    kc_map = lambda b, h, j: (lat_blocks + b, h // kv_group)
    in_specs, args = [], []
    if n_maps == 2:
        in_specs.append(pl.BlockSpec(lam_vec.shape, lambda b, h, j: (0, 0)))
        args.append(lam_vec)
    in_specs += [pl.BlockSpec((bq, HEAD), q_map),
                 pl.BlockSpec((ctx, HEAD), kc_map), pl.BlockSpec((ctx, HEAD), kc_map)]
    args += [q, k, v]
    if not ctx_queries:
        kl_map = lambda b, h, j: (b, h // kv_group)
        in_specs += [pl.BlockSpec((seq, HEAD), kl_map), pl.BlockSpec((seq, HEAD), kl_map)]
        args += [k, v]
    if n_maps == 2:
        in_specs.append(pl.BlockSpec((1, HEAD), lambda b, h, j: (0, 0)))
        args.append(sub_g)
    o_map = (lambda b, h, j: (b, h)) if ctx_queries else (lambda b, h, j: (b * nq + j, h))
    scratch = [pltpu.VMEM((2, min(sub_rows, bq), ctx if ctx_queries else ctx + seq), F32)]
    return pl.pallas_call(
        functools.partial(_attn_kernel, n_maps=n_maps, with_lat=not ctx_queries, lam_init=lam_init,
                          key_chunk=min(key_chunk, seq), sub_rows=min(sub_rows, bq)),
        grid=(batch, n_heads, nq),
        in_specs=in_specs,
        out_specs=pl.BlockSpec((bq, HEAD), o_map),
        out_shape=jax.ShapeDtypeStruct((m_out, n_heads * HEAD), BF16),
        scratch_shapes=scratch,
        compiler_params=_params("parallel", "parallel", "arbitrary"),
        name="attn_ctx" if ctx_queries else "attn_lat",
    )(*args)


def _na_window_start(kind, a):
    if kind == 0:
        return max(a - NA_ROWS // 2, 0), 0
    if kind == 1:
        return a, NA_ROWS // 2
    return min(a + NA_ROWS // 2, NA_WROWS - NA_ROWS), NA_WROWS - NA_QROWS


def _na_build_bias(tiles_ref, bias_ref, kind):
    w = GRID_W
    lane = lax.broadcasted_iota(jnp.int32, (w, 2 * w), 1)
    masked = jnp.full((w, 2 * w), MASKED, F32)
    for a in range(NA_QROWS):
        lo, off = _na_window_start(kind, a)
        for c in range(0, NA_WROWS, 2):
            ok0 = lo <= c < lo + NA_ROWS
            ok1 = lo <= c + 1 < lo + NA_ROWS
            if ok0 or ok1:
                i0 = c - a - off + NA_ROWS - 1
                assert 0 <= i0 <= 2 * NA_ROWS - 2
                tile = tiles_ref[i0]
                if not ok1:
                    tile = jnp.where(lane < w, tile, MASKED)
                elif not ok0:
                    tile = jnp.where(lane >= w, tile, MASKED)
            else:
                tile = masked
            bias_ref[kind, a * w:(a + 1) * w, c * w:(c + 2) * w] = tile


def _na_kernel(q_ref, k_ref, v_ref, kc_ref, vc_ref, tiles_ref, o_ref, bias_ref, s_ref, *, n_rows, blocks,
               key_chunk=ATTN_KEY_CHUNK, sub_rows=ATTN_SUB_ROWS):
    b = pl.program_id(1)
    step = pl.program_id(2)
    nb = n_rows // NA_QROWS
    wlen = NA_WROWS * GRID_W
    block_rows = NA_QROWS * GRID_W
    per_block = block_rows // sub_rows

    for kind, jj in ((0, 0), (1, 1), (2, nb - 1)):
        @pl.when(jnp.logical_and(b == 0, step == jj // blocks))
        def _():
            _na_build_bias(tiles_ref, bias_ref, kind)

    kinds, starts = [], []
    for t in range(blocks):
        j = step * blocks + t
        kinds.append(jnp.where(j == 0, 0, jnp.where(j == nb - 1, 2, 1)))
        starts.append(pl.multiple_of(
            jnp.clip(NA_QROWS * j - NA_ROWS // 2, 0, n_rows - NA_WROWS) * GRID_W, 4 * GRID_W))
    n_ctx = kc_ref.shape[0]
    n_items = q_ref.shape[0] // sub_rows
    segments = [(False, 0, n_ctx, 0)]
    segments += [(True, c * key_chunk, key_chunk, n_ctx + c * key_chunk) for c in range(wlen // key_chunk)]

    def score(i, seg):
        window, off, n, _ = seg
        q = q_ref[i * sub_rows:(i + 1) * sub_rows, :]
        if not window:
            return _dot_nt(q, kc_ref[...])
        t, r = divmod(i, per_block)
        k = k_ref[pl.ds(starts[t] + off, n), :]
        return _dot_nt(q, k) + bias_ref[kinds[t], r * sub_rows:(r + 1) * sub_rows, off:off + n]

    def value(i, seg):
        window, off, n, _ = seg
        return v_ref[pl.ds(starts[i // per_block] + off, n), :] if window else vc_ref[...]

    outs = _softmax_pipeline(n_items, sub_rows, segments, score, value, s_ref)
    for i, o in enumerate(outs):
        o_ref[i * sub_rows:(i + 1) * sub_rows, :] = o.astype(o_ref.dtype)


def _na_tiles_kernel(rpb_ref, o_ref):
    n = o_ref.shape[1]
    shift = GRID_W.bit_length() - 1
    jj = lax.broadcasted_iota(jnp.int32, (LANES, n), 0)
    pos = lax.broadcasted_iota(jnp.int32, (LANES, n), 1)
    rel = (pos & (GRID_W - 1)) - (pos >> shift) + (NA_COLS - 1)
    onehot = (rel == jj).astype(F32)
    vals = jnp.dot(rpb_ref[...], onehot, preferred_element_type=F32, precision=lax.Precision.HIGHEST)
    pos1 = lax.broadcasted_iota(jnp.int32, (1, n), 1)
    qc = pos1 >> shift
    kc = pos1 & (GRID_W - 1)
    d = kc - jnp.clip(qc - NA_COLS // 2, 0, GRID_W - NA_COLS)
    inside = jnp.logical_and(d >= 0, d < NA_COLS)
    o_ref[...] = jnp.where(inside, vals * LOG2E, MASKED)


def _na_tiles(rpb):
    n_heads, n_ri, n_ci = rpb.shape
    assert n_ri == 2 * NA_ROWS - 1 and n_ci == 2 * NA_COLS - 1 and n_ci <= LANES and n_ri < 2 * SUBLANES
    padded = jnp.pad(rpb.astype(F32), ((0, 0), (0, 2 * SUBLANES - n_ri), (0, LANES - n_ci)))
    n = GRID_W * GRID_W
    flat = pl.pallas_call(
        _na_tiles_kernel,
        grid=(n_heads,),
        in_specs=[pl.BlockSpec((None, 2 * SUBLANES, LANES), lambda h: (h, 0, 0))],
        out_specs=pl.BlockSpec((None, 2 * SUBLANES, n), lambda h: (h, 0, 0)),
        out_shape=jax.ShapeDtypeStruct((n_heads, 2 * SUBLANES, n), F32),
        compiler_params=_params("parallel"),
        name="na_tiles",
    )(padded)
    t = flat.reshape(n_heads, 2 * SUBLANES, GRID_W, GRID_W)
    return jnp.concatenate([t[:, :n_ri], t[:, 1:n_ri + 1]], axis=-1)


def _neighbourhood(q, k, v, rpb, rows):
    batch, seq, ctx = rows.batch, rows.seq, rows.ctx
    n_rows = seq // GRID_W
    assert n_rows % NA_QROWS == 0 and n_rows >= NA_WROWS
    nb = n_rows // NA_QROWS
    bq = NA_QROWS * GRID_W
    wlen = NA_WROWS * GRID_W
    lat_blocks = rows.n_lat // ctx
    tiles = _na_tiles(rpb)
    blocks = next(c for c in (4, 2, 1) if nb % c == 0)
    ns = nb // blocks
    return pl.pallas_call(
        functools.partial(_na_kernel, n_rows=n_rows, blocks=blocks),
        grid=(N_HEADS, batch, ns),
        in_specs=[pl.BlockSpec((blocks * bq, HEAD), lambda h, b, j: (b * ns + j, h)),
                  pl.BlockSpec((seq, HEAD), lambda h, b, j: (b, h)),
                  pl.BlockSpec((seq, HEAD), lambda h, b, j: (b, h)),
                  pl.BlockSpec((ctx, HEAD), lambda h, b, j: (lat_blocks + b, h)),
                  pl.BlockSpec((ctx, HEAD), lambda h, b, j: (lat_blocks + b, h)),
                  pl.BlockSpec((None,) + tiles.shape[1:], lambda h, b, j: (h, 0, 0, 0))],
        out_specs=pl.BlockSpec((blocks * bq, HEAD), lambda h, b, j: (b * ns + j, h)),
        out_shape=jax.ShapeDtypeStruct((rows.n_lat, N_HEADS * HEAD), BF16),
        scratch_shapes=[pltpu.VMEM((3, bq, wlen), F32), pltpu.VMEM((2, ATTN_SUB_ROWS, ctx + wlen), F32)],
        compiler_params=_params("arbitrary", "arbitrary", "arbitrary"),
        name="na_attn",
    )(q, k, v, k, v, tiles)


def _conv_kernel(prev_ref, cur_ref, next_ref, w_ref, b_ref, o_ref, ext_ref, *, blocks_per_seq, n_lat_blocks):
    i = pl.program_id(1)
    bm = cur_ref.shape[0]
    pos = i % blocks_per_seq
    is_lat = i < n_lat_blocks
    first = jnp.logical_or(jnp.logical_not(is_lat), pos == 0)
    last = jnp.logical_or(jnp.logical_not(is_lat), pos == blocks_per_seq - 1)
    halo_p = prev_ref[...].astype(F32)
    halo_n = next_ref[...].astype(F32)
    ext_ref[0:SUBLANES, :] = jnp.where(first, jnp.zeros_like(halo_p), halo_p)
    ext_ref[SUBLANES:SUBLANES + bm, :] = cur_ref[...].astype(F32)
    ext_ref[SUBLANES + bm:2 * SUBLANES + bm, :] = jnp.where(last, jnp.zeros_like(halo_n), halo_n)
    w = w_ref[...]
    acc = jnp.zeros(o_ref.shape, F32) + b_ref[...]
    for t in range(C_CONV):
        off = SUBLANES - C_CONV // 2 + t
        acc = acc + ext_ref[off:off + bm, :] * w[t:t + 1, :]
    o_ref[...] = _silu(acc).astype(o_ref.dtype)


def _conv_silu(xbc, conv_w, conv_b, rows, bc=2048):
    m, n = xbc.shape
    bm = rows.ctx
    assert rows.seq % bm == 0 and bm % SUBLANES == 0
    hb = bm // SUBLANES
    n8 = m // SUBLANES
    return pl.pallas_call(
        functools.partial(_conv_kernel, blocks_per_seq=rows.seq // bm, n_lat_blocks=rows.n_lat // bm),
        grid=(n // bc, m // bm),
        in_specs=[pl.BlockSpec((SUBLANES, bc), lambda j, i: (jnp.maximum(i * hb - 1, 0), j)),
                  pl.BlockSpec((bm, bc), lambda j, i: (i, j)),
                  pl.BlockSpec((SUBLANES, bc), lambda j, i: (jnp.minimum((i + 1) * hb, n8 - 1), j)),
                  pl.BlockSpec((C_CONV, bc), lambda j, i: (0, j)),
                  pl.BlockSpec((1, bc), lambda j, i: (0, j))],
        out_specs=pl.BlockSpec((bm, bc), lambda j, i: (i, j)),
        out_shape=jax.ShapeDtypeStruct((m, n), BF16),
        scratch_shapes=[pltpu.VMEM((bm + 2 * SUBLANES, bc), F32)],
        compiler_params=_params("parallel", "arbitrary"),
        name="conv_silu",
    )(xbc, xbc, xbc, conv_w, conv_b.reshape(1, n))


def _ssd_chunk(x_ref, b_ref, c_ref, dt_ref, alog_ref, dtb_ref, dsk_ref, y_ref, h_ref, forward):
    q = C_CHUNK
    half = q // 2
    assert q == 2 * C_HD and C_HPG % 2 == 0
    row = lax.broadcasted_iota(jnp.int32, (q, q), 0)
    col = lax.broadcasted_iota(jnp.int32, (q, q), 1)
    before = (col <= row) if forward else (col >= row)

    def thrice(mask):
        m = jnp.where(mask, 1.0, 0.0).astype(BF16)
        return jnp.concatenate([m, m, m], axis=1)

    def pieces(v):
        hi = v.astype(BF16).astype(F32)
        r = v - hi
        mid = r.astype(BF16).astype(F32)
        return hi, mid, r - mid

    def per_channel(v):
        rep = lambda t: jnp.concatenate([jnp.broadcast_to(t[h:h + 1, :], (C_HD, q)) for h in range(C_HPG)], axis=0)
        return jnp.concatenate([rep(t).astype(BF16) for t in pieces(v)], axis=1)

    dt_row = dt_ref[...] + dtb_ref[...]
    dt_row = jnp.maximum(dt_row, 0.0) + jnp.log1p(jnp.exp(-jnp.abs(dt_row)))
    dta_row = dt_row * (-jnp.exp(alog_ref[...]))
    yield
    before3 = thrice(before)
    cs_e = _dot_nt(before3, per_channel(dta_row))
    dt_e = _dot_nt(thrice(row == col), per_channel(dt_row))
    pad = jnp.zeros_like(dta_row)
    dta16 = jnp.concatenate([jnp.concatenate([t, pad], axis=0).astype(BF16) for t in pieces(dta_row)], axis=1)
    cs_row = _dot_nt(dta16, before3)[:C_HPG]
    end_e = cs_e[q - 1:q, :] if forward else cs_e[0:1, :]
    yield

    x = x_ref[...].astype(F32)
    xdt = x * dt_e
    bc = b_ref[...]
    cc = c_ref[...]
    cb = _dot_nt(cc, bc)
    h_old = h_ref[...]
    y = _dot(cc, h_old.astype(BF16)) * jnp.exp(cs_e) + dsk_ref[...] * x
    yield

    lane = lax.broadcasted_iota(jnp.int32, (1, q), 1)
    same_half = (row < half) == (col < half)
    parts = []
    for h0 in range(0, C_HPG, 2):
        cp = cs_e[:, h0 * C_HD:(h0 + 2) * C_HD]
        r1 = jnp.where(lane < half, cs_row[h0:h0 + 1, :], cs_row[h0 + 1:h0 + 2, :])
        r2 = jnp.where(lane < half, cs_row[h0 + 1:h0 + 2, :], cs_row[h0:h0 + 1, :])
        u1 = cb * jnp.exp(jnp.where(before, cp - r1, MASKED))
        u2 = cb * jnp.exp(jnp.where(before, pltpu.roll(cp, half, 1) - r2, MASKED))
        xp = xdt[:, h0 * C_HD:(h0 + 2) * C_HD]
        rhs = jnp.concatenate([jnp.where(same_half, xp, 0.0), jnp.where(same_half, 0.0, xp)], axis=0)
        parts.append(_dot(jnp.concatenate([u1, u2], axis=1).astype(BF16), rhs.astype(BF16)))
        yield
    y_ref[...] = (y + jnp.concatenate(parts, axis=1)).astype(y_ref.dtype)

    bct = bc.astype(F32).T.astype(BF16)
    h_ref[...] = h_old * jnp.exp(end_e) + _dot(bct, (xdt * jnp.exp(end_e - cs_e)).astype(BF16))


def _ssd_kernel(*refs):
    y_refs, h_ref = refs[14:16], refs[16]
    hp = C_HPG * C_HD

    @pl.when(pl.program_id(2) == 0)
    def _():
        h_ref[...] = jnp.zeros_like(h_ref)

    chains = []
    for d in range(2):
        x_ref, b_ref, c_ref, dt_ref, alog_ref, dtb_ref, dsk_ref = refs[7 * d:7 * d + 7]
        for g in range(SSD_GROUPS_PER_STEP):
            wide = (slice(None), pl.ds(g * hp, hp))
            state = (slice(None), pl.ds(g * C_STATE, C_STATE))
            head = (pl.ds(g * C_HPG, C_HPG), slice(None))
            chains.append(_ssd_chunk(x_ref.at[wide], b_ref.at[state], c_ref.at[state], dt_ref.at[head],
                                     alog_ref.at[head], dtb_ref.at[head], dsk_ref.at[wide], y_refs[d].at[wide],
                                     h_ref.at[d, g], d == 0))
    while chains:
        chains = [c for c in chains if next(c, StopIteration) is not StopIteration]


def _ssd(xbc, dt_t, a_log, dt_bias, d_skip, rows):
    batch, seq, ctx = rows.batch, rows.seq, rows.ctx
    m = xbc.shape[0]
    q = C_CHUNK
    nc_ctx = ctx // q
    nc_lat = seq // q
    nc = nc_ctx + nc_lat
    lat_chunks = rows.n_lat // q
    hp = C_HPG * C_HD

    def chunk(d):
        def index(b, s):
            c_ctx = s if d == 0 else nc_ctx - 1 - s
            c_lat = s - nc_ctx if d == 0 else nc - 1 - s
            return jnp.where(s < nc_ctx, lat_chunks + b * nc_ctx + c_ctx, b * nc_lat + c_lat)
        return index

    gps = SSD_GROUPS_PER_STEP
    ng = C_GROUPS // gps
    n_xb = C_DI // (gps * C_STATE)
    n_bb = ng
    heads = 2 * C_GROUPS * C_HPG
    col = lambda t: t.astype(F32).reshape(heads, 1)
    dsk_e = jnp.repeat(d_skip.astype(F32).reshape(2, 1, C_GROUPS * C_HPG), C_HD, axis=2)
    in_specs, args, out_specs = [], [], []
    for d in range(2):
        ch = chunk(d)
        hspec = pl.BlockSpec((gps * C_HPG, 1), lambda b, g, s, d=d: (d * ng + g, 0))
        in_specs += [pl.BlockSpec((q, gps * hp), lambda b, g, s, ch=ch: (ch(b, s), g)),
                     pl.BlockSpec((q, gps * C_STATE), lambda b, g, s, ch=ch: (ch(b, s), n_xb + g)),
                     pl.BlockSpec((q, gps * C_STATE), lambda b, g, s, ch=ch: (ch(b, s), n_xb + n_bb + g)),
                     pl.BlockSpec((gps * C_HPG, q), lambda b, g, s, ch=ch, d=d: (d * ng + g, ch(b, s))),
                     hspec, hspec,
                     pl.BlockSpec((None, 1, gps * hp), lambda b, g, s, d=d: (d, 0, g))]
        args += [xbc, xbc, xbc, dt_t, col(a_log), col(dt_bias), dsk_e]
        out_specs.append(pl.BlockSpec((q, gps * hp), lambda b, g, s, ch=ch: (ch(b, s), g)))
    return pl.pallas_call(
        _ssd_kernel,
        grid=(batch, ng, nc),
        in_specs=in_specs,
        out_specs=out_specs,
        out_shape=[jax.ShapeDtypeStruct((m, C_DI), BF16)] * 2,
        scratch_shapes=[pltpu.VMEM((2, gps, C_STATE, hp), F32)],
        compiler_params=_params("parallel", "parallel", "arbitrary"),
        name="ssd_scan",
    )(*args)


def _gate_norm_kernel(yf_ref, yb_ref, z_ref, g_ref, o_ref):
    v = (yf_ref[...].astype(F32) + yb_ref[...].astype(F32)) * _silu(z_ref[...].astype(F32))
    o_ref[...] = (v * lax.rsqrt(jnp.mean(v * v, axis=1, keepdims=True) + EPS) * g_ref[...]).astype(o_ref.dtype)


def _gate_norm(y_fwd, y_bwd, z, norm_g, bm=256):
    m, n = z.shape
    blk = pl.BlockSpec((bm, n), lambda i: (i, 0))
    return pl.pallas_call(
        _gate_norm_kernel,
        grid=(m // bm,),
        in_specs=[blk, blk, blk, pl.BlockSpec((1, n), lambda i: (0, 0))],
        out_specs=blk,
        out_shape=jax.ShapeDtypeStruct((m, n), BF16),
        compiler_params=_params("parallel"),
        name="gate_norm",
    )(y_fwd, y_bwd, z, norm_g.astype(F32).reshape(1, n))


def _rope_tables(seq, dim, n_copies, bm):
    t = jnp.arange(seq, dtype=jnp.int32)
    row = (t // GRID_W).astype(F32)
    colp = (t % GRID_W).astype(F32)
    n_pairs = dim // 4
    inv = ROPE_BASE ** (-jnp.arange(n_pairs, dtype=F32) / n_pairs)
    ang = jnp.concatenate([row[:, None] * inv, colp[:, None] * inv], axis=-1)
    cos = jnp.tile(jnp.cos(ang), (1, 2 * n_copies))
    sin = jnp.tile(jnp.sin(ang), (1, n_copies))
    sin = jnp.concatenate([-sin, sin], axis=-1)
    cos = jnp.concatenate([cos, jnp.ones((bm, HEAD), F32)], axis=0)
    sin = jnp.concatenate([sin, jnp.zeros((bm, HEAD), F32)], axis=0)
    return cos, sin


def _deinterleave_perm(n_maps):
    dim = HEAD // n_maps
    perm = []
    for e in range(2):
        for mi in range(n_maps):
            for p in range(dim // 2):
                perm.append(mi * dim + 2 * p + e)
    return perm


def _head_perm(n_heads, n_maps):
    base = _deinterleave_perm(n_maps)
    return jnp.asarray([h * HEAD + c for h in range(n_heads) for c in base], dtype=jnp.int32)


def _mixer_a(u, w_in, lam_vec, sub_g, rows, lam_init, need_ctx):
    d = D_MODEL
    perm = _head_perm(N_HEADS, 2)
    w_q = w_in[:, :d][:, perm].astype(BF16)
    w_k = w_in[:, d:2 * d][:, perm].astype(BF16)
    rope = _rope_tables(rows.seq, A_DK, 2, rows.bm)
    qh = _proj(u, w_q, 0, d, rows, mode="rope", rope=rope, scale=A_DK ** -0.5 * LOG2E)
    kh = _proj(u, w_k, 0, d, rows, mode="rope", rope=rope)
    vh = _proj(u, w_in.astype(BF16), 2 * d, d, rows)
    kw = dict(kv_group=1, n_maps=2, lam_vec=lam_vec.astype(F32), sub_g=sub_g.astype(F32).reshape(1, HEAD),
              lam_init=lam_init)
    o_l = _attention(qh, kh, vh, rows, **kw)
    if not need_ctx:
        return o_l
    return o_l, _attention(qh, kh, vh, rows, ctx_queries=True, **kw)


def _mixer_b(u, w_in, qn_g, kn_g, rows, need_ctx):
    d = D_MODEL
    dkv = B_KV_HEADS * HEAD
    base = jnp.asarray(_deinterleave_perm(1), dtype=jnp.int32)
    w_q = w_in[:, :d][:, _head_perm(N_HEADS, 1)].astype(BF16)
    w_k = w_in[:, d:d + dkv][:, _head_perm(B_KV_HEADS, 1)].astype(BF16)
    rope = _rope_tables(rows.seq, HEAD, 1, rows.bm)
    qh = _proj(u, w_q, 0, d, rows, mode="rope", rope=rope, scale=HEAD ** -0.5 * LOG2E,
               gain=qn_g.astype(F32)[base].reshape(1, HEAD))
    kh = _proj(u, w_k, 0, dkv, rows, mode="rope", rope=rope, gain=kn_g.astype(F32)[base].reshape(1, HEAD), bn=dkv)
    vh = _proj(u, w_in.astype(BF16), d + dkv, dkv, rows, bn=dkv)
    o_l = _attention(qh, kh, vh, rows, kv_group=B_GROUP, blocks_per_step=8)
    if not need_ctx:
        return o_l
    return o_l, _attention(qh, kh, vh, rows, kv_group=B_GROUP, ctx_queries=True)


def _mixer_c(u, w_in, conv_w, conv_b, a_log, dt_bias, d_skip, norm_g, rows):
    w = w_in.astype(BF16)
    n_dt = w_in.shape[1] - C_DI - C_CONV_DIM
    z = _proj(u, w, 0, C_DI, rows)
    xbc = _proj(u, w, C_DI, C_CONV_DIM, rows)
    dt = _proj(u, w, C_DI + C_CONV_DIM, n_dt, rows, out_dtype=F32, bn=n_dt)
    xbc = _conv_silu(xbc, conv_w.astype(F32), conv_b.astype(F32), rows)
    y_fwd, y_bwd = _ssd(xbc, dt.T, a_log, dt_bias, d_skip, rows)
    return _gate_norm(y_fwd, y_bwd, z, norm_g)


def _mixer_d(u, w_in, rpb, rows, need_ctx):
    d = D_MODEL
    w = w_in.astype(BF16)
    qh = _proj(u, w, 0, d, rows, scale=HEAD ** -0.5 * LOG2E)
    kh = _proj(u, w, d, d, rows)
    vh = _proj(u, w, 2 * d, d, rows)
    o_l = _neighbourhood(qh, kh, vh, rpb, rows)
    if not need_ctx:
        return o_l
    return o_l, _attention(qh, kh, vh, rows, kv_group=1, ctx_queries=True)


def _row_block(seq, ctx_total, target=1024):
    bm = math.gcd(seq, ctx_total)
    while bm > target and bm % 2 == 0:
        bm //= 2
    return bm


def kernel(x, c, ctx, c_ctx, ada_w, ada_b, ln_g, ln_b, mlp_w1, mlp_w2, a_w_in, a_lambda, a_sub_g, a_w_out, b_w_in, b_qn_g, b_kn_g, b_w_out, c_w_in, c_conv_w, c_conv_b, c_A_log, c_dt_bias, c_D, c_norm_g, c_w_out, d_w_in, d_rpb, d_w_out):
    batch, seq, d = x.shape
    n_ctx = ctx.shape[1]
    depth = ada_w.shape[0]
    assert d == D_MODEL and batch < SUBLANES and seq % GRID_W == 0
    rows = _Rows(batch, seq, n_ctx, _row_block(seq, batch * n_ctx))

    cvec = jnp.zeros((SUBLANES, d), F32).at[:batch].set(c.astype(F32)).at[batch].set(c_ctx.astype(F32))
    mods = _ada_mods(cvec, ada_w.astype(F32), ada_b.astype(F32))
    x_all = jnp.concatenate([x.reshape(batch * seq, d), ctx.reshape(batch * n_ctx, d)], axis=0).astype(F32)
    u = _modulate(x_all, mods, 0, rows)

    for i in range(depth):
        mixer, j = i % N_MIXERS, i // N_MIXERS
        last = i == depth - 1
        if mixer == 0:
            o = _mixer_a(u, a_w_in[j], a_lambda[j], a_sub_g[j], rows, 0.8 - 0.6 * math.exp(-0.3 * i), not last)
            w_out = a_w_out[j]
        elif mixer == 1:
            o = _mixer_b(u, b_w_in[j], b_qn_g[j], b_kn_g[j], rows, not last)
            w_out = b_w_out[j]
        elif mixer == 2:
            o = _mixer_c(u, c_w_in[j], c_conv_w[j], c_conv_b[j], c_A_log[j], c_dt_bias[j], c_D[j], c_norm_g[j], rows)
            w_out = c_w_out[j]
        else:
            o = _mixer_d(u, d_w_in[j], d_rpb[j], rows, not last)
            w_out = d_w_out[j]
        n_rows = rows.n_lat if last else rows.n_all
        x_all, u = _mm_ln(o, w_out.astype(BF16), x_all, mods, i, 2, ln_g[i, 0].astype(F32), ln_b[i, 0].astype(F32),
                          rows, next_mod=(i, 3, 4), n_rows=n_rows)
        hidden = _proj(u, mlp_w1[i].astype(BF16), 0, D_FF, rows, mode="relu2", n_rows=n_rows)
        x_all, u = _mm_ln(hidden, mlp_w2[i].astype(BF16), x_all, mods, i, 5, ln_g[i, 1].astype(F32),
                          ln_b[i, 1].astype(F32), rows, next_mod=None if last else (i + 1, 0, 1), n_rows=n_rows)
    return x_all.reshape(batch, seq, d).astype(x.dtype)
```
